```python
import jax, jax.numpy as jnp
from jax import lax
import numpy as np

D_MODEL = 2048
BATCH = 4
SEQ = 8192
DEPTH = 1

CHUNK = 64
GMLP_BLOCK = 128
A_WIDTH = D_MODEL // 2
A_GROUPS = 8
A_HEAD = A_WIDTH // A_GROUPS
B_WIDTH = D_MODEL // 2
CONV_WIDTH = 31
D_FF = ((8 * D_MODEL // 3 + 255) // 256) * 256
LN_EPS = 1e-5
ALPHA = (2.0 * DEPTH) ** 0.25
BETA = (8.0 * DEPTH) ** -0.25
IN_COLS = 2 * A_WIDTH + 2 * B_WIDTH + 2 * D_MODEL

kernel_name = "hybrid_gmlp_conformer_conv_macaron_deepnorm"


def layer_norm(x, g, b):
    xf = x.astype(jnp.float32)
    mu = jnp.mean(xf, axis=-1, keepdims=True)
    var = jnp.mean(jnp.square(xf - mu), axis=-1, keepdims=True)
    y = (xf - mu) * lax.rsqrt(var + LN_EPS)
    return (y * g.astype(jnp.float32) + b.astype(jnp.float32)).astype(x.dtype)


def swiglu(x, w_gu, w_down):
    gate, up = jnp.split(x @ w_gu, 2, axis=-1)
    return (jax.nn.silu(gate) * up) @ w_down


def block_causal_mask(n):
    c = jnp.arange(n) // CHUNK
    return c[None, :] <= c[:, None]


def spatial_gating(u, v, ln_g, ln_b, w_s, b_s):
    bsz, seq, _ = v.shape
    v = layer_norm(v, ln_g, ln_b)
    nblk = seq // GMLP_BLOCK
    vb = v.reshape(bsz, nblk, GMLP_BLOCK, A_GROUPS, A_HEAD)
    w = jnp.where(block_causal_mask(GMLP_BLOCK)[None], w_s, jnp.zeros((), w_s.dtype))
    s = jnp.einsum('hij,bnjhd->bnihd', w, vb) + jnp.transpose(b_s)[None, None, :, :, None]
    return u * s.reshape(bsz, seq, A_WIDTH)


def conv_module(h, w_dw, b_dw, ln_g, ln_b):
    a, g = jnp.split(h, 2, axis=-1)
    z = a * jax.nn.sigmoid(g)
    z = lax.conv_general_dilated(
        z, w_dw[:, None, :], window_strides=(1,),
        padding=[(CONV_WIDTH - 1, 0)],
        dimension_numbers=('NWC', 'WIO', 'NWC'),
        feature_group_count=B_WIDTH) + b_dw
    z = layer_norm(z, ln_g, ln_b)
    return jax.nn.silu(z)


def setup_inputs(seed: int = 0) -> dict:
    key = jax.random.key(seed)
    ks = jax.random.split(key, 32)
    n = jax.random.normal
    L = DEPTH
    def gain(k, d):
        return 1.0 + 0.01 * n(k, (L, d), jnp.float32)
    def bias(k, shape):
        return 0.01 * n(k, (L,) + shape, jnp.float32)
    return {
        "x": n(ks[0], (BATCH, SEQ, D_MODEL), jnp.float32),
        "ffn1_w_gu": n(ks[1], (L, D_MODEL, 2 * D_FF), jnp.float32) * D_MODEL ** -0.5,
        "ffn1_w_down": n(ks[2], (L, D_FF, D_MODEL), jnp.float32) * (D_FF ** -0.5 * BETA),
        "ln1_g": gain(ks[3], D_MODEL),
        "ln1_b": bias(ks[4], (D_MODEL,)),
        "w_in": n(ks[5], (L, D_MODEL, IN_COLS), jnp.float32) * D_MODEL ** -0.5,
        "b_in": bias(ks[6], (IN_COLS,)),
        "sgu_ln_g": gain(ks[7], A_WIDTH),
        "sgu_ln_b": bias(ks[8], (A_WIDTH,)),
        "sgu_w_s": n(ks[9], (L, A_GROUPS, GMLP_BLOCK, GMLP_BLOCK), jnp.float32) * (0.5 * GMLP_BLOCK ** -0.5),
        "sgu_b_s": 1.0 + 0.01 * n(ks[10], (L, A_GROUPS, GMLP_BLOCK), jnp.float32),
        "w_a_proj": n(ks[11], (L, A_WIDTH, D_MODEL), jnp.float32) * (A_WIDTH ** -0.5 * BETA),
        "conv_w_dw": n(ks[12], (L, CONV_WIDTH, B_WIDTH), jnp.float32) * CONV_WIDTH ** -0.5,
        "conv_b_dw": bias(ks[13], (B_WIDTH,)),
        "conv_ln_g": gain(ks[14], B_WIDTH),
        "conv_ln_b": bias(ks[15], (B_WIDTH,)),
        "w_b_proj": n(ks[16], (L, B_WIDTH, D_MODEL), jnp.float32) * (B_WIDTH ** -0.5 * BETA),
        "w_out": n(ks[17], (L, D_MODEL, D_MODEL), jnp.float32) * (D_MODEL ** -0.5 * BETA),
        "ln2_g": gain(ks[18], D_MODEL),
        "ln2_b": bias(ks[19], (D_MODEL,)),
        "ffn2_w_gu": n(ks[20], (L, D_MODEL, 2 * D_FF), jnp.float32) * D_MODEL ** -0.5,
        "ffn2_w_down": n(ks[21], (L, D_FF, D_MODEL), jnp.float32) * (D_FF ** -0.5 * BETA),
        "ln3_g": gain(ks[22], D_MODEL),
        "ln3_b": bias(ks[23], (D_MODEL,)),
    }


def reference(x, ffn1_w_gu, ffn1_w_down, ln1_g, ln1_b, w_in, b_in, sgu_ln_g, sgu_ln_b,
              sgu_w_s, sgu_b_s, w_a_proj, conv_w_dw, conv_b_dw, conv_ln_g, conv_ln_b,
              w_b_proj, w_out, ln2_g, ln2_b, ffn2_w_gu, ffn2_w_down, ln3_g, ln3_b):
    for l in range(DEPTH):
        x = layer_norm(ALPHA * x + 0.5 * swiglu(x, ffn1_w_gu[l], ffn1_w_down[l]), ln1_g[l], ln1_b[l])

        proj = x @ w_in[l] + b_in[l]
        u_a, v_a, h_b, gate_logits = jnp.split(
            proj, [A_WIDTH, 2 * A_WIDTH, 2 * A_WIDTH + 2 * B_WIDTH], axis=-1)

        y_a = spatial_gating(jax.nn.gelu(u_a), jax.nn.gelu(v_a), sgu_ln_g[l], sgu_ln_b[l],
                             sgu_w_s[l], sgu_b_s[l]) @ w_a_proj[l]
        y_b = conv_module(h_b, conv_w_dw[l], conv_b_dw[l], conv_ln_g[l], conv_ln_b[l]) @ w_b_proj[l]

        g_a, g_b = jnp.split(jax.nn.sigmoid(gate_logits), 2, axis=-1)
        mix = (g_a * y_a + g_b * y_b) @ w_out[l]
        x = layer_norm(ALPHA * x + mix, ln2_g[l], ln2_b[l])

        x = layer_norm(ALPHA * x + 0.5 * swiglu(x, ffn2_w_gu[l], ffn2_w_down[l]), ln3_g[l], ln3_b[l])
    return x
```

```python
import functools

import jax
import jax.numpy as jnp
from jax import lax
from jax.experimental import pallas as pl
from jax.experimental.pallas import tpu as pltpu

D_MODEL = 2048
DEPTH = 1
CHUNK = 64
GMLP_BLOCK = 128
A_WIDTH = D_MODEL // 2
A_GROUPS = 8
A_HEAD = A_WIDTH // A_GROUPS
B_WIDTH = D_MODEL // 2
CONV_WIDTH = 31
D_FF = ((8 * D_MODEL // 3 + 255) // 256) * 256
LN_EPS = 1e-5
ALPHA = (2.0 * DEPTH) ** 0.25

F32 = jnp.float32
BF16 = jnp.bfloat16

V7X_LANES = 128
V7X_VMEM_LIMIT_BYTES = 60000 * 1024

FFN_ROWS = 1024
FFN_COLS = 512
MIX_ROWS = 512
MERGE_COLS = 512
LN_ROWS = 64
CONV_ROWS = 64
CONV_HALO = 32


def _layer_norm_rows(src_ref, dst_ref, g, b, post=None):
    rows = src_ref.shape[0]

    def body(r, carry):
        sl = pl.ds(pl.multiple_of(r * LN_ROWS, LN_ROWS), LN_ROWS)
        x = src_ref[sl, :].astype(F32)
        mu = jnp.mean(x, axis=-1, keepdims=True)
        xc = x - mu
        var = jnp.mean(xc * xc, axis=-1, keepdims=True)
        y = xc * lax.rsqrt(var + LN_EPS) * g + b
        if post is not None:
            y = post(y)
        dst_ref[sl, :] = y.astype(dst_ref.dtype)
        return carry

    lax.fori_loop(0, rows // LN_ROWS, body, 0)


def _ffn_kernel(x_ref, wg_ref, wu_ref, wd_ref, g_ref, b_ref, o_ref, xb_ref):
    j = pl.program_id(1)

    @pl.when(j == 0)
    def _():
        x = x_ref[...]
        xb_ref[...] = x.astype(BF16)
        o_ref[...] = ALPHA * x

    xb = xb_ref[...]
    hg = jnp.dot(xb, wg_ref[...], preferred_element_type=F32)
    hu = jnp.dot(xb, wu_ref[...], preferred_element_type=F32)
    act = (0.5 * (jax.nn.silu(hg) * hu)).astype(BF16)
    o_ref[...] += jnp.dot(act, wd_ref[...], preferred_element_type=F32)

    @pl.when(j == pl.num_programs(1) - 1)
    def _():
        _layer_norm_rows(o_ref, o_ref, g_ref[...], b_ref[...])


def _ffn(x, w_gu, w_down, g, b):
    m = x.shape[0]
    nj = D_FF // FFN_COLS
    return pl.pallas_call(
        _ffn_kernel,
        grid=(m // FFN_ROWS, nj),
        in_specs=[
            pl.BlockSpec((FFN_ROWS, D_MODEL), lambda i, j: (i, 0)),
            pl.BlockSpec((D_MODEL, FFN_COLS), lambda i, j: (0, j)),
            pl.BlockSpec((D_MODEL, FFN_COLS), lambda i, j: (0, j + nj)),
            pl.BlockSpec((FFN_COLS, D_MODEL), lambda i, j: (j, 0)),
            pl.BlockSpec((1, D_MODEL), lambda i, j: (0, 0)),
            pl.BlockSpec((1, D_MODEL), lambda i, j: (0, 0)),
        ],
        out_specs=pl.BlockSpec((FFN_ROWS, D_MODEL), lambda i, j: (i, 0)),
        out_shape=jax.ShapeDtypeStruct((m, D_MODEL), F32),
        scratch_shapes=[pltpu.VMEM((FFN_ROWS, D_MODEL), BF16)],
        compiler_params=pltpu.CompilerParams(
            dimension_semantics=("parallel", "arbitrary"),
            vmem_limit_bytes=V7X_VMEM_LIMIT_BYTES),
        name="ffn_swiglu_ln",
    )(x, w_gu, w_gu, w_down, g, b)


def _mixer_branches_kernel(tiles_per_seq, x_ref, w_ref, bin_ref, sg_ref, sb_ref,
                           ws_ref, bs_ref, wdw_ref, bdw_ref, cg_ref, cb_ref,
                           a_out, b_out, xb_ref, u_ref, t_ref, vb_ref, zbuf_ref):
    i = pl.program_id(0)
    j = pl.program_id(1)
    tm = x_ref.shape[0]

    @pl.when(j == 0)
    def _():
        xb_ref[...] = x_ref[...].astype(BF16)

    proj = jnp.dot(xb_ref[...], w_ref[...], preferred_element_type=F32) + bin_ref[...]

    @pl.when(j == 0)
    def _():
        u_ref[...] = jax.nn.gelu(proj, approximate=True)

    @pl.when(j == 1)
    def _():
        t_ref[...] = jax.nn.gelu(proj, approximate=True)
        _layer_norm_rows(t_ref, vb_ref, sg_ref[...], sb_ref[...])
        row = lax.broadcasted_iota(jnp.int32, (GMLP_BLOCK, GMLP_BLOCK), 0)
        col = lax.broadcasted_iota(jnp.int32, (GMLP_BLOCK, GMLP_BLOCK), 1)
        keep = (col // CHUNK) <= (row // CHUNK)
        for h in range(A_GROUPS):
            wm = jnp.where(keep, ws_ref[h], 0.0).astype(BF16)
            bias = bs_ref[h]
            cs = slice(h * A_HEAD, (h + 1) * A_HEAD)
            for blk in range(tm // GMLP_BLOCK):
                rs = slice(blk * GMLP_BLOCK, (blk + 1) * GMLP_BLOCK)
                s = jnp.dot(wm, vb_ref[rs, cs], preferred_element_type=F32) + bias
                a_out[rs, cs] = (u_ref[rs, cs] * s).astype(a_out.dtype)

    @pl.when(j == 2)
    def _():
        t_ref[...] = proj

    @pl.when(j == 3)
    def _():
        first = (i % tiles_per_seq) == 0

        @pl.when(first)
        def _():
            zbuf_ref[0:CONV_HALO, :] = jnp.zeros((CONV_HALO, B_WIDTH), F32)

        @pl.when(jnp.logical_not(first))
        def _():
            zbuf_ref[0:CONV_HALO, :] = zbuf_ref[tm:tm + CONV_HALO, :]

        zbuf_ref[CONV_HALO:CONV_HALO + tm, :] = t_ref[...] * jax.nn.sigmoid(proj)

        off = CONV_HALO - (CONV_WIDTH - 1)

        def conv_body(c, carry):
            cs = pl.ds(pl.multiple_of(c * V7X_LANES, V7X_LANES), V7X_LANES)
            taps = [wdw_ref[k:k + 1, cs] for k in range(CONV_WIDTH)]
            bias = bdw_ref[:, cs]
            for r in range(tm // CONV_ROWS):
                base = r * CONV_ROWS
                acc = jnp.zeros((CONV_ROWS, V7X_LANES), F32)
                for k in range(CONV_WIDTH):
                    acc = acc + taps[k] * zbuf_ref[base + off + k:base + off + k + CONV_ROWS, cs]
                t_ref[base:base + CONV_ROWS, cs] = acc + bias
            return carry

        lax.fori_loop(0, B_WIDTH // V7X_LANES, conv_body, 0)
        _layer_norm_rows(t_ref, b_out, cg_ref[...], cb_ref[...], post=jax.nn.silu)


def _mixer_branches(x, w_in, b_in, sgu_g, sgu_b, w_s, b_s, w_dw, b_dw, conv_g, conv_b, seq):
    m = x.shape[0]
    tm = MIX_ROWS
    full = lambda shape: pl.BlockSpec(shape, lambda i, j: (0,) * len(shape))
    kernel = functools.partial(_mixer_branches_kernel, seq // tm)
    return pl.pallas_call(
        kernel,
        grid=(m // tm, 4),
        in_specs=[
            pl.BlockSpec((tm, D_MODEL), lambda i, j: (i, 0)),
            pl.BlockSpec((D_MODEL, A_WIDTH), lambda i, j: (0, j)),
            pl.BlockSpec((1, A_WIDTH), lambda i, j: (0, j)),
            full((1, A_WIDTH)), full((1, A_WIDTH)),
            full((A_GROUPS, GMLP_BLOCK, GMLP_BLOCK)),
            full((A_GROUPS, GMLP_BLOCK, A_HEAD)),
            full((CONV_WIDTH, B_WIDTH)), full((1, B_WIDTH)),
            full((1, B_WIDTH)), full((1, B_WIDTH)),
        ],
        out_specs=[
            pl.BlockSpec((tm, A_WIDTH), lambda i, j: (i, 0)),
            pl.BlockSpec((tm, B_WIDTH), lambda i, j: (i, 0)),
        ],
        out_shape=[
            jax.ShapeDtypeStruct((m, A_WIDTH), BF16),
            jax.ShapeDtypeStruct((m, B_WIDTH), BF16),
        ],
        scratch_shapes=[
            pltpu.VMEM((tm, D_MODEL), BF16),
            pltpu.VMEM((tm, A_WIDTH), F32),
            pltpu.VMEM((tm, A_WIDTH), F32),
            pltpu.VMEM((tm, A_WIDTH), BF16),
            pltpu.VMEM((tm + CONV_HALO, B_WIDTH), F32),
        ],
        compiler_params=pltpu.CompilerParams(
            dimension_semantics=("arbitrary", "arbitrary"),
            vmem_limit_bytes=V7X_VMEM_LIMIT_BYTES),
        name="mixer_branches",
    )(x, w_in, b_in, sgu_g, sgu_b, w_s, b_s, w_dw, b_dw, conv_g, conv_b)


def _mixer_merge_kernel(x_ref, a_ref, b_ref, wa_ref, wb_ref, wga_ref, wgb_ref,
                        bga_ref, bgb_ref, wo_ref, g_ref, bt_ref, o_ref, xb_ref):
    j = pl.program_id(1)

    @pl.when(j == 0)
    def _():
        x = x_ref[...]
        xb_ref[...] = x.astype(BF16)
        o_ref[...] = ALPHA * x

    xb = xb_ref[...]
    ya = jnp.dot(a_ref[...], wa_ref[...], preferred_element_type=F32)
    yb = jnp.dot(b_ref[...], wb_ref[...], preferred_element_type=F32)
    ga = jax.nn.sigmoid(jnp.dot(xb, wga_ref[...], preferred_element_type=F32) + bga_ref[...])
    gb = jax.nn.sigmoid(jnp.dot(xb, wgb_ref[...], preferred_element_type=F32) + bgb_ref[...])
    mix = (ga * ya + gb * yb).astype(BF16)
    o_ref[...] += jnp.dot(mix, wo_ref[...], preferred_element_type=F32)

    @pl.when(j == pl.num_programs(1) - 1)
    def _():
        _layer_norm_rows(o_ref, o_ref, g_ref[...], bt_ref[...])


def _mixer_merge(x, a, b, w_a, w_b, w_in, b_in, w_out, g, bt):
    m = x.shape[0]
    tm = MIX_ROWS
    tc = MERGE_COLS
    nj = D_MODEL // tc
    ga0 = (2 * A_WIDTH + 2 * B_WIDTH) // tc
    gb0 = ga0 + nj
    return pl.pallas_call(
        _mixer_merge_kernel,
        grid=(m // tm, nj),
        in_specs=[
            pl.BlockSpec((tm, D_MODEL), lambda i, j: (i, 0)),
            pl.BlockSpec((tm, A_WIDTH), lambda i, j: (i, 0)),
            pl.BlockSpec((tm, B_WIDTH), lambda i, j: (i, 0)),
            pl.BlockSpec((A_WIDTH, tc), lambda i, j: (0, j)),
            pl.BlockSpec((B_WIDTH, tc), lambda i, j: (0, j)),
            pl.BlockSpec((D_MODEL, tc), lambda i, j: (0, ga0 + j)),
            pl.BlockSpec((D_MODEL, tc), lambda i, j: (0, gb0 + j)),
            pl.BlockSpec((1, tc), lambda i, j: (0, ga0 + j)),
            pl.BlockSpec((1, tc), lambda i, j: (0, gb0 + j)),
            pl.BlockSpec((tc, D_MODEL), lambda i, j: (j, 0)),
            pl.BlockSpec((1, D_MODEL), lambda i, j: (0, 0)),
            pl.BlockSpec((1, D_MODEL), lambda i, j: (0, 0)),
        ],
        out_specs=pl.BlockSpec((tm, D_MODEL), lambda i, j: (i, 0)),
        out_shape=jax.ShapeDtypeStruct((m, D_MODEL), F32),
        scratch_shapes=[pltpu.VMEM((tm, D_MODEL), BF16)],
        compiler_params=pltpu.CompilerParams(
            dimension_semantics=("parallel", "arbitrary"),
            vmem_limit_bytes=V7X_VMEM_LIMIT_BYTES),
        name="mixer_merge",
    )(x, a, b, w_a, w_b, w_in, w_in, b_in, b_in, w_out, g, bt)


def kernel(x, ffn1_w_gu, ffn1_w_down, ln1_g, ln1_b, w_in, b_in, sgu_ln_g, sgu_ln_b, sgu_w_s, sgu_b_s, w_a_proj, conv_w_dw, conv_b_dw, conv_ln_g, conv_ln_b, w_b_proj, w_out, ln2_g, ln2_b, ffn2_w_gu, ffn2_w_down, ln3_g, ln3_b):
    bsz, seq, _ = x.shape
    h = x.reshape(bsz * seq, D_MODEL)
    row = lambda v: v.reshape(1, -1)
    for l in range(DEPTH):
        h = _ffn(h, ffn1_w_gu[l].astype(BF16), ffn1_w_down[l].astype(BF16),
                 row(ln1_g[l]), row(ln1_b[l]))
        w_in_b = w_in[l].astype(BF16)
        b_in_r = row(b_in[l])
        b_s = jnp.broadcast_to(sgu_b_s[l][:, :, None], (A_GROUPS, GMLP_BLOCK, A_HEAD))
        a, b = _mixer_branches(h, w_in_b, b_in_r, row(sgu_ln_g[l]), row(sgu_ln_b[l]),
                               sgu_w_s[l], b_s, conv_w_dw[l], row(conv_b_dw[l]),
                               row(conv_ln_g[l]), row(conv_ln_b[l]), seq)
        h = _mixer_merge(h, a, b, w_a_proj[l].astype(BF16), w_b_proj[l].astype(BF16),
                         w_in_b, b_in_r, w_out[l].astype(BF16), row(ln2_g[l]), row(ln2_b[l]))
        h = _ffn(h, ffn2_w_gu[l].astype(BF16), ffn2_w_down[l].astype(BF16),
                 row(ln3_g[l]), row(ln3_b[l]))
    return h.reshape(bsz, seq, D_MODEL)
```

```python
import functools

import jax
import jax.numpy as jnp
from jax import lax
from jax.experimental import pallas as pl
from jax.experimental.pallas import tpu as pltpu

D_MODEL = 2048
DEPTH = 1
CHUNK = 64
GMLP_BLOCK = 128
A_WIDTH = D_MODEL // 2
A_GROUPS = 8
A_HEAD = A_WIDTH // A_GROUPS
B_WIDTH = D_MODEL // 2
CONV_WIDTH = 31
D_FF = ((8 * D_MODEL // 3 + 255) // 256) * 256
LN_EPS = 1e-5
ALPHA = (2.0 * DEPTH) ** 0.25

F32 = jnp.float32
BF16 = jnp.bfloat16

V7X_LANES = 128
V7X_MXU_COLS = 256
V7X_VMEM_LIMIT_BYTES = 60000 * 1024

FFN_ROWS = 1024
FFN_COLS = 512
MIX_ROWS = 512
MIX_PROJ_COLS = 512
MERGE_COLS = 512
LN_ROWS = 64
EPILOGUE_ROWS = 256
CONV_ROWS = 64
CONV_HALO = 32
CONV_LANE_CHUNKS = B_WIDTH // V7X_LANES


def _layer_norm(x, g, b):
    mu = jnp.mean(x, axis=-1, keepdims=True)
    xc = x - mu
    var = jnp.mean(xc * xc, axis=-1, keepdims=True)
    return xc * lax.rsqrt(var + LN_EPS) * g + b


def _layer_norm_rows(src_ref, dst_ref, g, b, post=None):
    rows = src_ref.shape[0]

    def body(r, carry):
        sl = pl.ds(pl.multiple_of(r * LN_ROWS, LN_ROWS), LN_ROWS)
        y = _layer_norm(src_ref[sl, :].astype(F32), g, b)
        if post is not None:
            y = post(y)
        dst_ref[sl, :] = y.astype(dst_ref.dtype)
        return carry

    lax.fori_loop(0, rows // LN_ROWS, body, 0, unroll=2)


def _layer_norm_rows_unrolled(src_ref, dst_ref, g, b, post=None):
    for r in range(src_ref.shape[0] // LN_ROWS):
        sl = slice(r * LN_ROWS, (r + 1) * LN_ROWS)
        y = _layer_norm(src_ref[sl, :].astype(F32), g, b)
        if post is not None:
            y = post(y)
        dst_ref[sl, :] = y.astype(dst_ref.dtype)


def _gelu(x):
    return jax.nn.gelu(x, approximate=True)


def _acc_layer_norm_rows(o_ref, lhs, w_ref, g, b):
    for r in range(o_ref.shape[0] // EPILOGUE_ROWS):
        rows = slice(r * EPILOGUE_ROWS, (r + 1) * EPILOGUE_ROWS)
        y = o_ref[rows, :] + jnp.dot(lhs[rows], w_ref[...], preferred_element_type=F32)
        for q in range(EPILOGUE_ROWS // LN_ROWS):
            sub = slice(q * LN_ROWS, (q + 1) * LN_ROWS)
            o_ref[r * EPILOGUE_ROWS + q * LN_ROWS:r * EPILOGUE_ROWS + (q + 1) * LN_ROWS, :] = (
                _layer_norm(y[sub], g, b))


def _ffn_kernel(x_ref, wg_ref, wu_ref, wd_ref, g_ref, b_ref, o_ref, xb_ref):
    j = pl.program_id(1)
    last = pl.num_programs(1) - 1

    def step(is_first, is_last):
        if is_first:
            xb_ref[...] = x_ref[...].astype(BF16)
        xb = xb_ref[...]
        hg = jnp.dot(xb, wg_ref[...], preferred_element_type=F32)
        hu = jnp.dot(xb, wu_ref[...], preferred_element_type=F32)
        act = (0.5 * (jax.nn.silu(hg) * hu)).astype(BF16)
        if is_last:
            _acc_layer_norm_rows(o_ref, act, wd_ref, g_ref[...], b_ref[...])
        else:
            d = jnp.dot(act, wd_ref[...], preferred_element_type=F32)
            if is_first:
                o_ref[...] = ALPHA * x_ref[...] + d
            else:
                o_ref[...] += d

    pl.when(j == 0)(functools.partial(step, True, False))
    pl.when(jnp.logical_and(j > 0, j < last))(functools.partial(step, False, False))
    pl.when(j == last)(functools.partial(step, False, True))


def _ffn(x, w_gu, w_down, g, b):
    m = x.shape[0]
    nj = D_FF // FFN_COLS
    return pl.pallas_call(
        _ffn_kernel,
        grid=(m // FFN_ROWS, nj),
        in_specs=[
            pl.BlockSpec((FFN_ROWS, D_MODEL), lambda i, j: (i, 0)),
            pl.BlockSpec((D_MODEL, FFN_COLS), lambda i, j: (0, j)),
            pl.BlockSpec((D_MODEL, FFN_COLS), lambda i, j: (0, j + nj)),
            pl.BlockSpec((FFN_COLS, D_MODEL), lambda i, j: (j, 0)),
            pl.BlockSpec((1, D_MODEL), lambda i, j: (0, 0)),
            pl.BlockSpec((1, D_MODEL), lambda i, j: (0, 0)),
        ],
        out_specs=pl.BlockSpec((FFN_ROWS, D_MODEL), lambda i, j: (i, 0)),
        out_shape=jax.ShapeDtypeStruct((m, D_MODEL), F32),
        scratch_shapes=[pltpu.VMEM((FFN_ROWS, D_MODEL), BF16)],
        compiler_params=pltpu.CompilerParams(
            dimension_semantics=("parallel", "arbitrary"),
            vmem_limit_bytes=V7X_VMEM_LIMIT_BYTES),
        name="ffn_swiglu_ln",
    )(x, w_gu, w_gu, w_down, g, b)


def _conv_lane_chunk(zbuf_ref, wdw_ref, bdw_ref, out_ref, c):
    tm = out_ref.shape[0]
    cs = slice(c * V7X_LANES, (c + 1) * V7X_LANES)
    first_off = CONV_HALO - (CONV_WIDTH - 1)
    bias = bdw_ref[:, cs]
    for rb in range(tm // CONV_ROWS):
        base = rb * CONV_ROWS
        acc = None
        for k in range(CONV_WIDTH):
            rows = pl.ds(base + first_off + k, CONV_ROWS, stride=1)
            term = wdw_ref[k:k + 1, cs] * zbuf_ref[c, rows, :]
            acc = term if acc is None else acc + term
        out_ref[base:base + CONV_ROWS, cs] = acc + bias


def _mixer_branches_kernel(tiles_per_seq, x_ref, wlo_ref, whi_ref, blo_ref, bhi_ref,
                           sg_ref, sb_ref, ws_ref, bs_ref, wdw_ref, bdw_ref, cg_ref, cb_ref,
                           a_out, b_out, xb_ref, u_ref, v_ref, vb_ref, zbuf_ref, conv_ref):
    i = pl.program_id(0)
    j = pl.program_id(1)
    tm = x_ref.shape[0]
    n_sub = MIX_PROJ_COLS // V7X_MXU_COLS
    lanes_per_sub = V7X_MXU_COLS // V7X_LANES
    halves = ((wlo_ref, blo_ref), (whi_ref, bhi_ref))

    def project(w_ref, b_ref, n):
        cs = slice(n * V7X_MXU_COLS, (n + 1) * V7X_MXU_COLS)
        return jnp.dot(xb_ref[...], w_ref[:, cs], preferred_element_type=F32) + b_ref[:, cs]

    def project_gelu(dst_ref, half, n):
        w_ref, b_ref = halves[half]
        p = project(w_ref, b_ref, n)
        col0 = half * MIX_PROJ_COLS + n * V7X_MXU_COLS
        for r in range(tm // LN_ROWS):
            rs = slice(r * LN_ROWS, (r + 1) * LN_ROWS)
            dst_ref[rs, col0:col0 + V7X_MXU_COLS] = _gelu(p[rs])

    def project_glu(lane0, n):
        seq_start = (i % tiles_per_seq) == 0
        a = project(wlo_ref, blo_ref, n)
        g = project(whi_ref, bhi_ref, n)
        for cc in range(lanes_per_sub):
            c = lane0 + n * lanes_per_sub + cc
            ls = slice(cc * V7X_LANES, (cc + 1) * V7X_LANES)
            tail = zbuf_ref[c, tm:tm + CONV_HALO, :]
            zbuf_ref[c, 0:CONV_HALO, :] = jnp.where(seq_start, 0.0, tail)
            for r in range(tm // LN_ROWS):
                rs = slice(r * LN_ROWS, (r + 1) * LN_ROWS)
                zs = slice(CONV_HALO + r * LN_ROWS, CONV_HALO + (r + 1) * LN_ROWS)
                zbuf_ref[c, zs, :] = a[rs, ls] * jax.nn.sigmoid(g[rs, ls])

    def conv(c):
        _conv_lane_chunk(zbuf_ref, wdw_ref, bdw_ref, conv_ref, c)

    @pl.when(j == 0)
    def _():
        @pl.when(i == 0)
        def _():
            zbuf_ref[...] = jnp.zeros_like(zbuf_ref)

        xb_ref[...] = x_ref[...].astype(BF16)
        project_glu(0, 0)
        project_glu(0, 1)
        conv(0)
        conv(1)

    @pl.when(j == 1)
    def _():
        project_glu(4, 0)
        conv(2)
        conv(3)
        project_glu(4, 1)
        conv(4)

    @pl.when(j == 2)
    def _():
        project_gelu(u_ref, 0, 0)
        project_gelu(u_ref, 0, 1)
        conv(5)
        project_gelu(u_ref, 1, 0)
        conv(6)
        project_gelu(u_ref, 1, 1)

    @pl.when(j == 3)
    def _():
        project_gelu(v_ref, 0, 0)
        project_gelu(v_ref, 0, 1)
        conv(7)
        project_gelu(v_ref, 1, 0)
        _layer_norm_rows_unrolled(conv_ref, b_out, cg_ref[...], cb_ref[...], post=jax.nn.silu)
        project_gelu(v_ref, 1, 1)
        _layer_norm_rows_unrolled(v_ref, vb_ref, sg_ref[...], sb_ref[...])
        row = lax.broadcasted_iota(jnp.int32, (GMLP_BLOCK, GMLP_BLOCK), 0)
        col = lax.broadcasted_iota(jnp.int32, (GMLP_BLOCK, GMLP_BLOCK), 1)
        keep = (col // CHUNK) <= (row // CHUNK)
        for h in range(A_GROUPS):
            wm = jnp.where(keep, ws_ref[h], 0.0).astype(BF16)
            bias = bs_ref[h]
            cs = slice(h * A_HEAD, (h + 1) * A_HEAD)
            for blk in range(tm // GMLP_BLOCK):
                rs = slice(blk * GMLP_BLOCK, (blk + 1) * GMLP_BLOCK)
                s = jnp.dot(wm, vb_ref[rs, cs], preferred_element_type=F32) + bias
                a_out[rs, cs] = (u_ref[rs, cs] * s).astype(a_out.dtype)


def _mixer_branches(x, w_in, b_in, sgu_g, sgu_b, w_s, b_s, w_dw, b_dw, conv_g, conv_b, seq):
    m = x.shape[0]
    tm = MIX_ROWS
    full = lambda shape: pl.BlockSpec(shape, lambda i, j: (0,) * len(shape))
    lo_blk = lambda i, j: (0, jnp.where(j < 2, 4 + j, 2 * j - 4))
    hi_blk = lambda i, j: (0, jnp.where(j < 2, 6 + j, 2 * j - 3))
    kernel = functools.partial(_mixer_branches_kernel, seq // tm)
    return pl.pallas_call(
        kernel,
        grid=(m // tm, 4),
        in_specs=[
            pl.BlockSpec((tm, D_MODEL), lambda i, j: (i, 0)),
            pl.BlockSpec((D_MODEL, MIX_PROJ_COLS), lo_blk),
            pl.BlockSpec((D_MODEL, MIX_PROJ_COLS), hi_blk),
            pl.BlockSpec((1, MIX_PROJ_COLS), lo_blk),
            pl.BlockSpec((1, MIX_PROJ_COLS), hi_blk),
            full((1, A_WIDTH)), full((1, A_WIDTH)),
            full((A_GROUPS, GMLP_BLOCK, GMLP_BLOCK)),
            full((A_GROUPS, GMLP_BLOCK, A_HEAD)),
            full((CONV_WIDTH, B_WIDTH)), full((1, B_WIDTH)),
            full((1, B_WIDTH)), full((1, B_WIDTH)),
        ],
        out_specs=[
            pl.BlockSpec((tm, A_WIDTH), lambda i, j: (i, 0)),
            pl.BlockSpec((tm, B_WIDTH), lambda i, j: (i, 0)),
        ],
        out_shape=[
            jax.ShapeDtypeStruct((m, A_WIDTH), BF16),
            jax.ShapeDtypeStruct((m, B_WIDTH), BF16),
        ],
        scratch_shapes=[
            pltpu.VMEM((tm, D_MODEL), BF16),
            pltpu.VMEM((tm, A_WIDTH), F32),
            pltpu.VMEM((tm, A_WIDTH), F32),
            pltpu.VMEM((tm, A_WIDTH), BF16),
            pltpu.VMEM((CONV_LANE_CHUNKS, tm + CONV_HALO, V7X_LANES), F32),
            pltpu.VMEM((tm, B_WIDTH), F32),
        ],
        compiler_params=pltpu.CompilerParams(
            dimension_semantics=("arbitrary", "arbitrary"),
            vmem_limit_bytes=V7X_VMEM_LIMIT_BYTES),
        name="mixer_branches",
    )(x, w_in, w_in, b_in, b_in, sgu_g, sgu_b, w_s, b_s, w_dw, b_dw, conv_g, conv_b)


def _mixer_merge_kernel(x_ref, a_ref, b_ref, wa_ref, wb_ref, wga_ref, wgb_ref,
                        bga_ref, bgb_ref, wo_ref, g_ref, bt_ref, o_ref, xb_ref):
    j = pl.program_id(1)
    last = pl.num_programs(1) - 1

    def step(is_first, is_last):
        if is_first:
            xb_ref[...] = x_ref[...].astype(BF16)
        xb = xb_ref[...]
        ya = jnp.dot(a_ref[...], wa_ref[...], preferred_element_type=F32)
        yb = jnp.dot(b_ref[...], wb_ref[...], preferred_element_type=F32)
        ga = jax.nn.sigmoid(jnp.dot(xb, wga_ref[...], preferred_element_type=F32) + bga_ref[...])
        gb = jax.nn.sigmoid(jnp.dot(xb, wgb_ref[...], preferred_element_type=F32) + bgb_ref[...])
        mix = (ga * ya + gb * yb).astype(BF16)
        if is_last:
            _acc_layer_norm_rows(o_ref, mix, wo_ref, g_ref[...], bt_ref[...])
        else:
            d = jnp.dot(mix, wo_ref[...], preferred_element_type=F32)
            if is_first:
                o_ref[...] = ALPHA * x_ref[...] + d
            else:
                o_ref[...] += d

    pl.when(j == 0)(functools.partial(step, True, False))
    pl.when(jnp.logical_and(j > 0, j < last))(functools.partial(step, False, False))
    pl.when(j == last)(functools.partial(step, False, True))


def _mixer_merge(x, a, b, w_a, w_b, w_in, b_in, w_out, g, bt):
    m = x.shape[0]
    tm = MIX_ROWS
    tc = MERGE_COLS
    nj = D_MODEL // tc
    ga0 = (2 * A_WIDTH + 2 * B_WIDTH) // tc
    gb0 = ga0 + nj
    return pl.pallas_call(
        _mixer_merge_kernel,
        grid=(m // tm, nj),
        in_specs=[
            pl.BlockSpec((tm, D_MODEL), lambda i, j: (i, 0)),
            pl.BlockSpec((tm, A_WIDTH), lambda i, j: (i, 0)),
            pl.BlockSpec((tm, B_WIDTH), lambda i, j: (i, 0)),
            pl.BlockSpec((A_WIDTH, tc), lambda i, j: (0, j)),
            pl.BlockSpec((B_WIDTH, tc), lambda i, j: (0, j)),
            pl.BlockSpec((D_MODEL, tc), lambda i, j: (0, ga0 + j)),
            pl.BlockSpec((D_MODEL, tc), lambda i, j: (0, gb0 + j)),
            pl.BlockSpec((1, tc), lambda i, j: (0, ga0 + j)),
            pl.BlockSpec((1, tc), lambda i, j: (0, gb0 + j)),
            pl.BlockSpec((tc, D_MODEL), lambda i, j: (j, 0)),
            pl.BlockSpec((1, D_MODEL), lambda i, j: (0, 0)),
            pl.BlockSpec((1, D_MODEL), lambda i, j: (0, 0)),
        ],
        out_specs=pl.BlockSpec((tm, D_MODEL), lambda i, j: (i, 0)),
        out_shape=jax.ShapeDtypeStruct((m, D_MODEL), F32),
        scratch_shapes=[pltpu.VMEM((tm, D_MODEL), BF16)],
        compiler_params=pltpu.CompilerParams(
            dimension_semantics=("parallel", "arbitrary"),
            vmem_limit_bytes=V7X_VMEM_LIMIT_BYTES),
        name="mixer_merge",
    )(x, a, b, w_a, w_b, w_in, w_in, b_in, b_in, w_out, g, bt)


def kernel(x, ffn1_w_gu, ffn1_w_down, ln1_g, ln1_b, w_in, b_in, sgu_ln_g, sgu_ln_b, sgu_w_s, sgu_b_s, w_a_proj, conv_w_dw, conv_b_dw, conv_ln_g, conv_ln_b, w_b_proj, w_out, ln2_g, ln2_b, ffn2_w_gu, ffn2_w_down, ln3_g, ln3_b):
    bsz, seq, _ = x.shape
    h = x.reshape(bsz * seq, D_MODEL)
    row = lambda v: v.reshape(1, -1)
    for l in range(DEPTH):
        h = _ffn(h, ffn1_w_gu[l].astype(BF16), ffn1_w_down[l].astype(BF16),
                 row(ln1_g[l]), row(ln1_b[l]))
        w_in_b = w_in[l].astype(BF16)
        b_in_r = row(b_in[l])
        b_s = jnp.broadcast_to(sgu_b_s[l][:, :, None], (A_GROUPS, GMLP_BLOCK, A_HEAD))
        a, b = _mixer_branches(h, w_in_b, b_in_r, row(sgu_ln_g[l]), row(sgu_ln_b[l]),
                               sgu_w_s[l], b_s, conv_w_dw[l], row(conv_b_dw[l]),
                               row(conv_ln_g[l]), row(conv_ln_b[l]), seq)
        h = _mixer_merge(h, a, b, w_a_proj[l].astype(BF16), w_b_proj[l].astype(BF16),
                         w_in_b, b_in_r, w_out[l].astype(BF16), row(ln2_g[l]), row(ln2_b[l]))
        h = _ffn(h, ffn2_w_gu[l].astype(BF16), ffn2_w_down[l].astype(BF16),
                 row(ln3_g[l]), row(ln3_b[l]))
    return h.reshape(bsz, seq, D_MODEL)
```

```python
import functools

import jax
import jax.numpy as jnp
from jax import lax
from jax.experimental import pallas as pl
from jax.experimental.pallas import tpu as pltpu

D_MODEL = 2048
DEPTH = 1
CHUNK = 64
GMLP_BLOCK = 128
A_WIDTH = D_MODEL // 2
A_GROUPS = 8
A_HEAD = A_WIDTH // A_GROUPS
B_WIDTH = D_MODEL // 2
CONV_WIDTH = 31
D_FF = ((8 * D_MODEL // 3 + 255) // 256) * 256
LN_EPS = 1e-5
ALPHA = (2.0 * DEPTH) ** 0.25

F32 = jnp.float32
BF16 = jnp.bfloat16

V7X_LANES = 128
V7X_MXU_COLS = 256
V7X_VMEM_LIMIT_BYTES = 60000 * 1024

FFN_ROWS = 1024
FFN_COLS = 512
MIX_ROWS = 512
MERGE_COLS = 512
LN_ROWS = 64
EPILOGUE_ROWS = 256
CONV_ROWS = 64
CONV_HALO = 32
CONV_LANE_CHUNKS = B_WIDTH // V7X_LANES


def _layer_norm(x, g, b):
    mu = jnp.mean(x, axis=-1, keepdims=True)
    xc = x - mu
    var = jnp.mean(xc * xc, axis=-1, keepdims=True)
    return xc * lax.rsqrt(var + LN_EPS) * g + b


def _layer_norm_rows(src_ref, dst_ref, g, b, post=None):
    rows = src_ref.shape[0]

    def body(r, carry):
        sl = pl.ds(pl.multiple_of(r * LN_ROWS, LN_ROWS), LN_ROWS)
        y = _layer_norm(src_ref[sl, :].astype(F32), g, b)
        if post is not None:
            y = post(y)
        dst_ref[sl, :] = y.astype(dst_ref.dtype)
        return carry

    lax.fori_loop(0, rows // LN_ROWS, body, 0, unroll=2)


def _layer_norm_rows_unrolled(src_ref, dst_ref, g, b, post=None):
    for r in range(src_ref.shape[0] // LN_ROWS):
        sl = slice(r * LN_ROWS, (r + 1) * LN_ROWS)
        y = _layer_norm(src_ref[sl, :].astype(F32), g, b)
        if post is not None:
            y = post(y)
        dst_ref[sl, :] = y.astype(dst_ref.dtype)


def _gelu(x):
    return jax.nn.gelu(x, approximate=True)


def _acc_layer_norm_rows(o_ref, lhs, w_ref, g, b):
    for r in range(o_ref.shape[0] // EPILOGUE_ROWS):
        rows = slice(r * EPILOGUE_ROWS, (r + 1) * EPILOGUE_ROWS)
        y = o_ref[rows, :] + jnp.dot(lhs[rows], w_ref[...], preferred_element_type=F32)
        for q in range(EPILOGUE_ROWS // LN_ROWS):
            sub = slice(q * LN_ROWS, (q + 1) * LN_ROWS)
            o_ref[r * EPILOGUE_ROWS + q * LN_ROWS:r * EPILOGUE_ROWS + (q + 1) * LN_ROWS, :] = (
                _layer_norm(y[sub], g, b))


def _ffn_kernel(x_ref, wg_ref, wu_ref, wd_ref, g_ref, b_ref, o_ref, xb_ref):
    j = pl.program_id(1)
    last = pl.num_programs(1) - 1

    def step(is_first, is_last):
        if is_first:
            xb_ref[...] = x_ref[...].astype(BF16)
        xb = xb_ref[...]
        hg = jnp.dot(xb, wg_ref[...], preferred_element_type=F32)
        hu = jnp.dot(xb, wu_ref[...], preferred_element_type=F32)
        act = (0.5 * (jax.nn.silu(hg) * hu)).astype(BF16)
        if is_last:
            _acc_layer_norm_rows(o_ref, act, wd_ref, g_ref[...], b_ref[...])
        else:
            d = jnp.dot(act, wd_ref[...], preferred_element_type=F32)
            if is_first:
                o_ref[...] = ALPHA * x_ref[...] + d
            else:
                o_ref[...] += d

    pl.when(j == 0)(functools.partial(step, True, False))
    pl.when(jnp.logical_and(j > 0, j < last))(functools.partial(step, False, False))
    pl.when(j == last)(functools.partial(step, False, True))


def _ffn(x, w_gu, w_down, g, b):
    m = x.shape[0]
    nj = D_FF // FFN_COLS
    return pl.pallas_call(
        _ffn_kernel,
        grid=(m // FFN_ROWS, nj),
        in_specs=[
            pl.BlockSpec((FFN_ROWS, D_MODEL), lambda i, j: (i, 0)),
            pl.BlockSpec((None, D_MODEL, FFN_COLS), lambda i, j: (j, 0, 0)),
            pl.BlockSpec((None, D_MODEL, FFN_COLS), lambda i, j: (j + nj, 0, 0)),
            pl.BlockSpec((FFN_COLS, D_MODEL), lambda i, j: (j, 0)),
            pl.BlockSpec((1, D_MODEL), lambda i, j: (0, 0)),
            pl.BlockSpec((1, D_MODEL), lambda i, j: (0, 0)),
        ],
        out_specs=pl.BlockSpec((FFN_ROWS, D_MODEL), lambda i, j: (i, 0)),
        out_shape=jax.ShapeDtypeStruct((m, D_MODEL), F32),
        scratch_shapes=[pltpu.VMEM((FFN_ROWS, D_MODEL), BF16)],
        compiler_params=pltpu.CompilerParams(
            dimension_semantics=("parallel", "arbitrary"),
            vmem_limit_bytes=V7X_VMEM_LIMIT_BYTES),
        name="ffn_swiglu_ln",
    )(x, w_gu, w_gu, w_down, g, b)


def _conv_lane_chunk(zbuf_ref, wdw_ref, bdw_ref, out_ref, c):
    tm = out_ref.shape[0]
    cs = slice(c * V7X_LANES, (c + 1) * V7X_LANES)
    first_off = CONV_HALO - (CONV_WIDTH - 1)
    bias = bdw_ref[:, cs]
    for rb in range(tm // CONV_ROWS):
        base = rb * CONV_ROWS
        acc = None
        for k in range(CONV_WIDTH):
            rows = pl.ds(base + first_off + k, CONV_ROWS, stride=1)
            term = wdw_ref[k:k + 1, cs] * zbuf_ref[c, rows, :]
            acc = term if acc is None else acc + term
        out_ref[base:base + CONV_ROWS, cs] = acc + bias


def _mixer_branches_kernel(tiles_per_seq, x_ref, w_ref, bin_ref,
                           sg_ref, sb_ref, ws_ref, bs_ref, wdw_ref, bdw_ref, cg_ref, cb_ref,
                           a_out, b_out, xb_ref, u_ref, v_ref, vb_ref, zbuf_ref, conv_ref):
    i = pl.program_id(0)
    tm = x_ref.shape[0]
    lanes_per_sub = V7X_MXU_COLS // V7X_LANES
    u_col, v_col, a_col, g_col = 0, A_WIDTH, 2 * A_WIDTH, 2 * A_WIDTH + B_WIDTH

    def project(col0):
        cs = slice(col0, col0 + V7X_MXU_COLS)
        return jnp.dot(xb_ref[...], w_ref[:, cs], preferred_element_type=F32) + bin_ref[:, cs]

    def project_gelu(dst_ref, col0, n):
        p = project(col0 + n * V7X_MXU_COLS)
        for r in range(tm // LN_ROWS):
            rs = slice(r * LN_ROWS, (r + 1) * LN_ROWS)
            dst_ref[rs, n * V7X_MXU_COLS:(n + 1) * V7X_MXU_COLS] = _gelu(p[rs])

    def project_glu(n):
        seq_start = (i % tiles_per_seq) == 0
        a = project(a_col + n * V7X_MXU_COLS)
        g = project(g_col + n * V7X_MXU_COLS)
        for cc in range(lanes_per_sub):
            c = n * lanes_per_sub + cc
            ls = slice(cc * V7X_LANES, (cc + 1) * V7X_LANES)
            tail = zbuf_ref[c, tm:tm + CONV_HALO, :]
            zbuf_ref[c, 0:CONV_HALO, :] = jnp.where(seq_start, 0.0, tail)
            for r in range(tm // LN_ROWS):
                rs = slice(r * LN_ROWS, (r + 1) * LN_ROWS)
                zs = slice(CONV_HALO + r * LN_ROWS, CONV_HALO + (r + 1) * LN_ROWS)
                zbuf_ref[c, zs, :] = a[rs, ls] * jax.nn.sigmoid(g[rs, ls])

    def conv(c):
        _conv_lane_chunk(zbuf_ref, wdw_ref, bdw_ref, conv_ref, c)

    @pl.when(i == 0)
    def _():
        zbuf_ref[...] = jnp.zeros_like(zbuf_ref)

    xb_ref[...] = x_ref[...].astype(BF16)
    project_glu(0)
    project_glu(1)
    conv(0)
    conv(1)
    project_glu(2)
    conv(2)
    conv(3)
    project_glu(3)
    conv(4)
    project_gelu(u_ref, u_col, 0)
    project_gelu(u_ref, u_col, 1)
    conv(5)
    project_gelu(u_ref, u_col, 2)
    conv(6)
    project_gelu(u_ref, u_col, 3)
    project_gelu(v_ref, v_col, 0)
    project_gelu(v_ref, v_col, 1)
    conv(7)
    project_gelu(v_ref, v_col, 2)
    _layer_norm_rows_unrolled(conv_ref, b_out, cg_ref[...], cb_ref[...], post=jax.nn.silu)
    project_gelu(v_ref, v_col, 3)
    _layer_norm_rows_unrolled(v_ref, vb_ref, sg_ref[...], sb_ref[...])
    row = lax.broadcasted_iota(jnp.int32, (GMLP_BLOCK, GMLP_BLOCK), 0)
    col = lax.broadcasted_iota(jnp.int32, (GMLP_BLOCK, GMLP_BLOCK), 1)
    keep = (col // CHUNK) <= (row // CHUNK)
    for h in range(A_GROUPS):
        wm = jnp.where(keep, ws_ref[h], 0.0).astype(BF16)
        bias = bs_ref[h]
        cs = slice(h * A_HEAD, (h + 1) * A_HEAD)
        for blk in range(tm // GMLP_BLOCK):
            rs = slice(blk * GMLP_BLOCK, (blk + 1) * GMLP_BLOCK)
            s = jnp.dot(wm, vb_ref[rs, cs], preferred_element_type=F32) + bias
            a_out[rs, cs] = (u_ref[rs, cs] * s).astype(a_out.dtype)


def _mixer_branches(x, w_in, b_in, sgu_g, sgu_b, w_s, b_s, w_dw, b_dw, conv_g, conv_b, seq):
    m = x.shape[0]
    tm = MIX_ROWS
    branch_cols = 2 * A_WIDTH + 2 * B_WIDTH
    full = lambda shape: pl.BlockSpec(shape, lambda i: (0,) * len(shape), pipeline_mode=pl.Buffered(1))
    kernel = functools.partial(_mixer_branches_kernel, seq // tm)
    return pl.pallas_call(
        kernel,
        grid=(m // tm,),
        in_specs=[
            pl.BlockSpec((tm, D_MODEL), lambda i: (i, 0)),
            full((D_MODEL, branch_cols)),
            full((1, branch_cols)),
            full((1, A_WIDTH)), full((1, A_WIDTH)),
            full((A_GROUPS, GMLP_BLOCK, GMLP_BLOCK)),
            full((A_GROUPS, GMLP_BLOCK, A_HEAD)),
            full((CONV_WIDTH, B_WIDTH)), full((1, B_WIDTH)),
            full((1, B_WIDTH)), full((1, B_WIDTH)),
        ],
        out_specs=[
            pl.BlockSpec((tm, A_WIDTH), lambda i: (i, 0)),
            pl.BlockSpec((tm, B_WIDTH), lambda i: (i, 0)),
        ],
        out_shape=[
            jax.ShapeDtypeStruct((m, A_WIDTH), BF16),
            jax.ShapeDtypeStruct((m, B_WIDTH), BF16),
        ],
        scratch_shapes=[
            pltpu.VMEM((tm, D_MODEL), BF16),
            pltpu.VMEM((tm, A_WIDTH), F32),
            pltpu.VMEM((tm, A_WIDTH), F32),
            pltpu.VMEM((tm, A_WIDTH), BF16),
            pltpu.VMEM((CONV_LANE_CHUNKS, tm + CONV_HALO, V7X_LANES), F32),
            pltpu.VMEM((tm, B_WIDTH), F32),
        ],
        compiler_params=pltpu.CompilerParams(
            dimension_semantics=("arbitrary",),
            vmem_limit_bytes=V7X_VMEM_LIMIT_BYTES),
        name="mixer_branches",
    )(x, w_in, b_in, sgu_g, sgu_b, w_s, b_s, w_dw, b_dw, conv_g, conv_b)


def _mixer_merge_kernel(x_ref, a_ref, b_ref, wa_ref, wb_ref, wga_ref, wgb_ref,
                        bga_ref, bgb_ref, wo_ref, g_ref, bt_ref, o_ref, xb_ref):
    j = pl.program_id(1)
    last = pl.num_programs(1) - 1

    def step(is_first, is_last):
        if is_first:
            xb_ref[...] = x_ref[...].astype(BF16)
        xb = xb_ref[...]
        ya = jnp.dot(a_ref[...], wa_ref[...], preferred_element_type=F32)
        yb = jnp.dot(b_ref[...], wb_ref[...], preferred_element_type=F32)
        ga = jax.nn.sigmoid(jnp.dot(xb, wga_ref[...], preferred_element_type=F32) + bga_ref[...])
        gb = jax.nn.sigmoid(jnp.dot(xb, wgb_ref[...], preferred_element_type=F32) + bgb_ref[...])
        mix = (ga * ya + gb * yb).astype(BF16)
        if is_last:
            _acc_layer_norm_rows(o_ref, mix, wo_ref, g_ref[...], bt_ref[...])
        else:
            d = jnp.dot(mix, wo_ref[...], preferred_element_type=F32)
            if is_first:
                o_ref[...] = ALPHA * x_ref[...] + d
            else:
                o_ref[...] += d

    pl.when(j == 0)(functools.partial(step, True, False))
    pl.when(jnp.logical_and(j > 0, j < last))(functools.partial(step, False, False))
    pl.when(j == last)(functools.partial(step, False, True))


def _mixer_merge(x, a, b, w_a, w_b, w_gates, b_in, w_out, g, bt):
    m = x.shape[0]
    tm = MIX_ROWS
    tc = MERGE_COLS
    nj = D_MODEL // tc
    ga0 = (2 * A_WIDTH + 2 * B_WIDTH) // tc
    gb0 = ga0 + nj
    return pl.pallas_call(
        _mixer_merge_kernel,
        grid=(m // tm, nj),
        in_specs=[
            pl.BlockSpec((tm, D_MODEL), lambda i, j: (i, 0)),
            pl.BlockSpec((tm, A_WIDTH), lambda i, j: (i, 0)),
            pl.BlockSpec((tm, B_WIDTH), lambda i, j: (i, 0)),
            pl.BlockSpec((None, A_WIDTH, tc), lambda i, j: (j, 0, 0)),
            pl.BlockSpec((None, B_WIDTH, tc), lambda i, j: (j, 0, 0)),
            pl.BlockSpec((None, D_MODEL, tc), lambda i, j: (j, 0, 0)),
            pl.BlockSpec((None, D_MODEL, tc), lambda i, j: (nj + j, 0, 0)),
            pl.BlockSpec((1, tc), lambda i, j: (0, ga0 + j)),
            pl.BlockSpec((1, tc), lambda i, j: (0, gb0 + j)),
            pl.BlockSpec((tc, D_MODEL), lambda i, j: (j, 0)),
            pl.BlockSpec((1, D_MODEL), lambda i, j: (0, 0)),
            pl.BlockSpec((1, D_MODEL), lambda i, j: (0, 0)),
        ],
        out_specs=pl.BlockSpec((tm, D_MODEL), lambda i, j: (i, 0)),
        out_shape=jax.ShapeDtypeStruct((m, D_MODEL), F32),
        scratch_shapes=[pltpu.VMEM((tm, D_MODEL), BF16)],
        compiler_params=pltpu.CompilerParams(
            dimension_semantics=("parallel", "arbitrary"),
            vmem_limit_bytes=V7X_VMEM_LIMIT_BYTES),
        name="mixer_merge",
    )(x, a, b, w_a, w_b, w_gates, w_gates, b_in, b_in, w_out, g, bt)


def kernel(x, ffn1_w_gu, ffn1_w_down, ln1_g, ln1_b, w_in, b_in, sgu_ln_g, sgu_ln_b, sgu_w_s, sgu_b_s, w_a_proj, conv_w_dw, conv_b_dw, conv_ln_g, conv_ln_b, w_b_proj, w_out, ln2_g, ln2_b, ffn2_w_gu, ffn2_w_down, ln3_g, ln3_b):
    bsz, seq, _ = x.shape
    h = x.reshape(bsz * seq, D_MODEL)
    row = lambda v: v.reshape(1, -1)

    def col_chunks(w, width):
        k, n = w.shape
        return w.astype(BF16).reshape(k, n // width, width).transpose(1, 0, 2)

    branch_cols = 2 * A_WIDTH + 2 * B_WIDTH
    for l in range(DEPTH):
        h = _ffn(h, col_chunks(ffn1_w_gu[l], FFN_COLS), ffn1_w_down[l].astype(BF16),
                 row(ln1_g[l]), row(ln1_b[l]))
        b_in_r = row(b_in[l])
        b_s = jnp.broadcast_to(sgu_b_s[l][:, :, None], (A_GROUPS, GMLP_BLOCK, A_HEAD))
        a, b = _mixer_branches(h, w_in[l][:, :branch_cols].astype(BF16), b_in_r,
                               row(sgu_ln_g[l]), row(sgu_ln_b[l]),
                               sgu_w_s[l], b_s, conv_w_dw[l], row(conv_b_dw[l]),
                               row(conv_ln_g[l]), row(conv_ln_b[l]), seq)
        h = _mixer_merge(h, a, b, col_chunks(w_a_proj[l], MERGE_COLS), col_chunks(w_b_proj[l], MERGE_COLS),
                         col_chunks(w_in[l][:, branch_cols:], MERGE_COLS), b_in_r,
                         w_out[l].astype(BF16), row(ln2_g[l]), row(ln2_b[l]))
        h = _ffn(h, col_chunks(ffn2_w_gu[l], FFN_COLS), ffn2_w_down[l].astype(BF16),
                 row(ln3_g[l]), row(ln3_b[l]))
    return h.reshape(bsz, seq, D_MODEL)
```

```python
import functools

import jax
import jax.numpy as jnp
from jax import lax
from jax.experimental import pallas as pl
from jax.experimental.pallas import tpu as pltpu

D_MODEL = 2048
DEPTH = 1
CHUNK = 64
GMLP_BLOCK = 128
A_WIDTH = D_MODEL // 2
A_GROUPS = 8
A_HEAD = A_WIDTH // A_GROUPS
B_WIDTH = D_MODEL // 2
CONV_WIDTH = 31
D_FF = ((8 * D_MODEL // 3 + 255) // 256) * 256
LN_EPS = 1e-5
ALPHA = (2.0 * DEPTH) ** 0.25

F32 = jnp.float32
BF16 = jnp.bfloat16

V7X_LANES = 128
V7X_MXU_COLS = 256
V7X_VMEM_LIMIT_BYTES = 60000 * 1024

FFN_ROWS = 1024
FFN_COLS = 512
MIX_ROWS = 512
LN_ROWS = 64
EPILOGUE_ROWS = 256
CONV_ROWS = 64
CONV_HALO = 32
CONV_LANE_CHUNKS = B_WIDTH // V7X_LANES


def _layer_norm(x, g, b):
    mu = jnp.mean(x, axis=-1, keepdims=True)
    xc = x - mu
    var = jnp.mean(xc * xc, axis=-1, keepdims=True)
    return xc * lax.rsqrt(var + LN_EPS) * g + b


def _layer_norm_rows(src_ref, dst_ref, g, b, post=None):
    rows = src_ref.shape[0]

    def body(r, carry):
        sl = pl.ds(pl.multiple_of(r * LN_ROWS, LN_ROWS), LN_ROWS)
        y = _layer_norm(src_ref[sl, :].astype(F32), g, b)
        if post is not None:
            y = post(y)
        dst_ref[sl, :] = y.astype(dst_ref.dtype)
        return carry

    lax.fori_loop(0, rows // LN_ROWS, body, 0, unroll=2)


def _layer_norm_rows_unrolled(src_ref, dst_ref, g, b, post=None):
    for r in range(src_ref.shape[0] // LN_ROWS):
        sl = slice(r * LN_ROWS, (r + 1) * LN_ROWS)
        y = _layer_norm(src_ref[sl, :].astype(F32), g, b)
        if post is not None:
            y = post(y)
        dst_ref[sl, :] = y.astype(dst_ref.dtype)


def _gelu(x):
    return jax.nn.gelu(x, approximate=True)


def _acc_layer_norm_rows(o_ref, lhs, w_ref, g, b):
    for r in range(o_ref.shape[0] // EPILOGUE_ROWS):
        rows = slice(r * EPILOGUE_ROWS, (r + 1) * EPILOGUE_ROWS)
        y = o_ref[rows, :] + jnp.dot(lhs[rows], w_ref[...], preferred_element_type=F32)
        for q in range(EPILOGUE_ROWS // LN_ROWS):
            sub = slice(q * LN_ROWS, (q + 1) * LN_ROWS)
            o_ref[r * EPILOGUE_ROWS + q * LN_ROWS:r * EPILOGUE_ROWS + (q + 1) * LN_ROWS, :] = (
                _layer_norm(y[sub], g, b))


def _ffn_kernel(x_ref, wg_ref, wu_ref, wd_ref, g_ref, b_ref, o_ref, xb_ref):
    j = pl.program_id(1)
    last = pl.num_programs(1) - 1

    def step(is_first, is_last):
        if is_first:
            xb_ref[...] = x_ref[...].astype(BF16)
        xb = xb_ref[...]
        hg = jnp.dot(xb, wg_ref[...], preferred_element_type=F32)
        hu = jnp.dot(xb, wu_ref[...], preferred_element_type=F32)
        act = (0.5 * (jax.nn.silu(hg) * hu)).astype(BF16)
        if is_last:
            _acc_layer_norm_rows(o_ref, act, wd_ref, g_ref[...], b_ref[...])
        else:
            d = jnp.dot(act, wd_ref[...], preferred_element_type=F32)
            if is_first:
                o_ref[...] = ALPHA * x_ref[...] + d
            else:
                o_ref[...] += d

    pl.when(j == 0)(functools.partial(step, True, False))
    pl.when(jnp.logical_and(j > 0, j < last))(functools.partial(step, False, False))
    pl.when(j == last)(functools.partial(step, False, True))


def _ffn(x, w_gu, w_down, g, b):
    m = x.shape[0]
    nj = D_FF // FFN_COLS
    return pl.pallas_call(
        _ffn_kernel,
        grid=(m // FFN_ROWS, nj),
        in_specs=[
            pl.BlockSpec((FFN_ROWS, D_MODEL), lambda i, j: (i, 0)),
            pl.BlockSpec((D_MODEL, FFN_COLS), lambda i, j: (0, j)),
            pl.BlockSpec((D_MODEL, FFN_COLS), lambda i, j: (0, j + nj)),
            pl.BlockSpec((FFN_COLS, D_MODEL), lambda i, j: (j, 0)),
            pl.BlockSpec((1, D_MODEL), lambda i, j: (0, 0)),
            pl.BlockSpec((1, D_MODEL), lambda i, j: (0, 0)),
        ],
        out_specs=pl.BlockSpec((FFN_ROWS, D_MODEL), lambda i, j: (i, 0)),
        out_shape=jax.ShapeDtypeStruct((m, D_MODEL), F32),
        scratch_shapes=[pltpu.VMEM((FFN_ROWS, D_MODEL), BF16)],
        compiler_params=pltpu.CompilerParams(
            dimension_semantics=("parallel", "arbitrary"),
            vmem_limit_bytes=V7X_VMEM_LIMIT_BYTES),
        name="ffn_swiglu_ln",
    )(x, w_gu, w_gu, w_down, g, b)


def _conv_lane_chunk(zbuf_ref, wdw_ref, bdw_ref, out_ref, c):
    tm = out_ref.shape[0]
    cs = slice(c * V7X_LANES, (c + 1) * V7X_LANES)
    first_off = CONV_HALO - (CONV_WIDTH - 1)
    bias = bdw_ref[:, cs]
    for rb in range(tm // CONV_ROWS):
        base = rb * CONV_ROWS
        acc = None
        for k in range(CONV_WIDTH):
            rows = pl.ds(base + first_off + k, CONV_ROWS, stride=1)
            term = wdw_ref[k:k + 1, cs] * zbuf_ref[c, rows, :]
            acc = term if acc is None else acc + term
        out_ref[base:base + CONV_ROWS, cs] = acc + bias


def _mixer_branches_kernel(tiles_per_seq, x_ref, w_ref, bin_ref,
                           sg_ref, sb_ref, ws_ref, bs_ref, wdw_ref, bdw_ref, cg_ref, cb_ref,
                           a_out, b_out, xb_ref, u_ref, v_ref, vb_ref, zbuf_ref, conv_ref):
    i = pl.program_id(0)
    tm = x_ref.shape[0]
    lanes_per_sub = V7X_MXU_COLS // V7X_LANES
    u_col, v_col, a_col, g_col = 0, A_WIDTH, 2 * A_WIDTH, 2 * A_WIDTH + B_WIDTH

    def project(col0):
        cs = slice(col0, col0 + V7X_MXU_COLS)
        return jnp.dot(xb_ref[...], w_ref[:, cs], preferred_element_type=F32) + bin_ref[:, cs]

    def project_gelu(dst_ref, col0, n):
        p = project(col0 + n * V7X_MXU_COLS)
        for r in range(tm // LN_ROWS):
            rs = slice(r * LN_ROWS, (r + 1) * LN_ROWS)
            dst_ref[rs, n * V7X_MXU_COLS:(n + 1) * V7X_MXU_COLS] = _gelu(p[rs])

    def project_glu(n):
        seq_start = (i % tiles_per_seq) == 0
        a = project(a_col + n * V7X_MXU_COLS)
        g = project(g_col + n * V7X_MXU_COLS)
        for cc in range(lanes_per_sub):
            c = n * lanes_per_sub + cc
            ls = slice(cc * V7X_LANES, (cc + 1) * V7X_LANES)
            tail = zbuf_ref[c, tm:tm + CONV_HALO, :]
            zbuf_ref[c, 0:CONV_HALO, :] = jnp.where(seq_start, 0.0, tail)
            for r in range(tm // LN_ROWS):
                rs = slice(r * LN_ROWS, (r + 1) * LN_ROWS)
                zs = slice(CONV_HALO + r * LN_ROWS, CONV_HALO + (r + 1) * LN_ROWS)
                zbuf_ref[c, zs, :] = a[rs, ls] * jax.nn.sigmoid(g[rs, ls])

    def conv(c):
        _conv_lane_chunk(zbuf_ref, wdw_ref, bdw_ref, conv_ref, c)

    @pl.when(i == 0)
    def _():
        zbuf_ref[...] = jnp.zeros_like(zbuf_ref)

    xb_ref[...] = x_ref[...].astype(BF16)
    project_glu(0)
    project_glu(1)
    conv(0)
    conv(1)
    project_glu(2)
    conv(2)
    conv(3)
    project_glu(3)
    conv(4)
    project_gelu(u_ref, u_col, 0)
    project_gelu(u_ref, u_col, 1)
    conv(5)
    project_gelu(u_ref, u_col, 2)
    conv(6)
    project_gelu(u_ref, u_col, 3)
    project_gelu(v_ref, v_col, 0)
    project_gelu(v_ref, v_col, 1)
    conv(7)
    project_gelu(v_ref, v_col, 2)
    _layer_norm_rows_unrolled(conv_ref, b_out, cg_ref[...], cb_ref[...], post=jax.nn.silu)
    project_gelu(v_ref, v_col, 3)
    _layer_norm_rows_unrolled(v_ref, vb_ref, sg_ref[...], sb_ref[...])
    row = lax.broadcasted_iota(jnp.int32, (GMLP_BLOCK, GMLP_BLOCK), 0)
    col = lax.broadcasted_iota(jnp.int32, (GMLP_BLOCK, GMLP_BLOCK), 1)
    keep = (col // CHUNK) <= (row // CHUNK)
    for h in range(A_GROUPS):
        wm = jnp.where(keep, ws_ref[h], 0.0).astype(BF16)
        bias = bs_ref[h]
        cs = slice(h * A_HEAD, (h + 1) * A_HEAD)
        for blk in range(tm // GMLP_BLOCK):
            rs = slice(blk * GMLP_BLOCK, (blk + 1) * GMLP_BLOCK)
            s = jnp.dot(wm, vb_ref[rs, cs], preferred_element_type=F32) + bias
            a_out[rs, cs] = (u_ref[rs, cs] * s).astype(a_out.dtype)


def _mixer_branches(x, w_in, b_in, sgu_g, sgu_b, w_s, b_s, w_dw, b_dw, conv_g, conv_b, seq):
    m = x.shape[0]
    tm = MIX_ROWS
    branch_cols = 2 * A_WIDTH + 2 * B_WIDTH
    full = lambda shape: pl.BlockSpec(shape, lambda i: (0,) * len(shape), pipeline_mode=pl.Buffered(1))
    kernel = functools.partial(_mixer_branches_kernel, seq // tm)
    return pl.pallas_call(
        kernel,
        grid=(m // tm,),
        in_specs=[
            pl.BlockSpec((tm, D_MODEL), lambda i: (i, 0)),
            full((D_MODEL, branch_cols)),
            full((1, branch_cols)),
            full((1, A_WIDTH)), full((1, A_WIDTH)),
            full((A_GROUPS, GMLP_BLOCK, GMLP_BLOCK)),
            full((A_GROUPS, GMLP_BLOCK, A_HEAD)),
            full((CONV_WIDTH, B_WIDTH)), full((1, B_WIDTH)),
            full((1, B_WIDTH)), full((1, B_WIDTH)),
        ],
        out_specs=[
            pl.BlockSpec((tm, A_WIDTH), lambda i: (i, 0)),
            pl.BlockSpec((tm, B_WIDTH), lambda i: (i, 0)),
        ],
        out_shape=[
            jax.ShapeDtypeStruct((m, A_WIDTH), BF16),
            jax.ShapeDtypeStruct((m, B_WIDTH), BF16),
        ],
        scratch_shapes=[
            pltpu.VMEM((tm, D_MODEL), BF16),
            pltpu.VMEM((tm, A_WIDTH), F32),
            pltpu.VMEM((tm, A_WIDTH), F32),
            pltpu.VMEM((tm, A_WIDTH), BF16),
            pltpu.VMEM((CONV_LANE_CHUNKS, tm + CONV_HALO, V7X_LANES), F32),
            pltpu.VMEM((tm, B_WIDTH), F32),
        ],
        compiler_params=pltpu.CompilerParams(
            dimension_semantics=("arbitrary",),
            vmem_limit_bytes=V7X_VMEM_LIMIT_BYTES),
        name="mixer_branches",
    )(x, w_in, b_in, sgu_g, sgu_b, w_s, b_s, w_dw, b_dw, conv_g, conv_b)


def _mixer_merge_kernel(x_ref, a_ref, b_ref, wa_ref, wb_ref, wg_ref, bg_ref, wo_ref, g_ref, bt_ref,
                        o_ref, xb_ref, mix_ref):
    tm = x_ref.shape[0]
    xb_ref[...] = x_ref[...].astype(BF16)
    for c in range(D_MODEL // V7X_MXU_COLS):
        cs = slice(c * V7X_MXU_COLS, (c + 1) * V7X_MXU_COLS)
        gs = slice(D_MODEL + c * V7X_MXU_COLS, D_MODEL + (c + 1) * V7X_MXU_COLS)
        ya = jnp.dot(a_ref[...], wa_ref[:, cs], preferred_element_type=F32)
        yb = jnp.dot(b_ref[...], wb_ref[:, cs], preferred_element_type=F32)
        la = jnp.dot(xb_ref[...], wg_ref[:, cs], preferred_element_type=F32) + bg_ref[:, cs]
        lb = jnp.dot(xb_ref[...], wg_ref[:, gs], preferred_element_type=F32) + bg_ref[:, gs]
        for r in range(tm // LN_ROWS):
            rs = slice(r * LN_ROWS, (r + 1) * LN_ROWS)
            mix = jax.nn.sigmoid(la[rs]) * ya[rs] + jax.nn.sigmoid(lb[rs]) * yb[rs]
            mix_ref[rs, cs] = mix.astype(BF16)
    g = g_ref[...]
    bt = bt_ref[...]
    for r in range(tm // EPILOGUE_ROWS):
        rows = slice(r * EPILOGUE_ROWS, (r + 1) * EPILOGUE_ROWS)
        y = ALPHA * x_ref[rows, :] + jnp.dot(mix_ref[rows, :], wo_ref[...], preferred_element_type=F32)
        for q in range(EPILOGUE_ROWS // LN_ROWS):
            sub = slice(q * LN_ROWS, (q + 1) * LN_ROWS)
            o_ref[r * EPILOGUE_ROWS + q * LN_ROWS:r * EPILOGUE_ROWS + (q + 1) * LN_ROWS, :] = (
                _layer_norm(y[sub], g, bt))


def _mixer_merge(x, a, b, w_a, w_b, w_gates, b_gates, w_out, g, bt):
    m = x.shape[0]
    tm = MIX_ROWS
    full = lambda shape: pl.BlockSpec(shape, lambda i: (0,) * len(shape), pipeline_mode=pl.Buffered(1))
    return pl.pallas_call(
        _mixer_merge_kernel,
        grid=(m // tm,),
        in_specs=[
            pl.BlockSpec((tm, D_MODEL), lambda i: (i, 0)),
            pl.BlockSpec((tm, A_WIDTH), lambda i: (i, 0)),
            pl.BlockSpec((tm, B_WIDTH), lambda i: (i, 0)),
            full((A_WIDTH, D_MODEL)),
            full((B_WIDTH, D_MODEL)),
            full((D_MODEL, 2 * D_MODEL)),
            full((1, 2 * D_MODEL)),
            full((D_MODEL, D_MODEL)),
            full((1, D_MODEL)),
            full((1, D_MODEL)),
        ],
        out_specs=pl.BlockSpec((tm, D_MODEL), lambda i: (i, 0)),
        out_shape=jax.ShapeDtypeStruct((m, D_MODEL), F32),
        scratch_shapes=[
            pltpu.VMEM((tm, D_MODEL), BF16),
            pltpu.VMEM((tm, D_MODEL), BF16),
        ],
        compiler_params=pltpu.CompilerParams(
            dimension_semantics=("parallel",),
            vmem_limit_bytes=V7X_VMEM_LIMIT_BYTES),
        name="mixer_merge",
    )(x, a, b, w_a, w_b, w_gates, b_gates, w_out, g, bt)


def kernel(x, ffn1_w_gu, ffn1_w_down, ln1_g, ln1_b, w_in, b_in, sgu_ln_g, sgu_ln_b, sgu_w_s, sgu_b_s, w_a_proj, conv_w_dw, conv_b_dw, conv_ln_g, conv_ln_b, w_b_proj, w_out, ln2_g, ln2_b, ffn2_w_gu, ffn2_w_down, ln3_g, ln3_b):
    bsz, seq, _ = x.shape
    h = x.reshape(bsz * seq, D_MODEL)
    row = lambda v: v.reshape(1, -1)

    branch_cols = 2 * A_WIDTH + 2 * B_WIDTH
    for l in range(DEPTH):
        h = _ffn(h, ffn1_w_gu[l].astype(BF16), ffn1_w_down[l].astype(BF16),
                 row(ln1_g[l]), row(ln1_b[l]))
        b_in_r = row(b_in[l])
        b_s = jnp.broadcast_to(sgu_b_s[l][:, :, None], (A_GROUPS, GMLP_BLOCK, A_HEAD))
        a, b = _mixer_branches(h, w_in[l][:, :branch_cols].astype(BF16), b_in_r,
                               row(sgu_ln_g[l]), row(sgu_ln_b[l]),
                               sgu_w_s[l], b_s, conv_w_dw[l], row(conv_b_dw[l]),
                               row(conv_ln_g[l]), row(conv_ln_b[l]), seq)
        h = _mixer_merge(h, a, b, w_a_proj[l].astype(BF16), w_b_proj[l].astype(BF16),
                         w_in[l][:, branch_cols:].astype(BF16), b_in_r[:, branch_cols:],
                         w_out[l].astype(BF16), row(ln2_g[l]), row(ln2_b[l]))
        h = _ffn(h, ffn2_w_gu[l].astype(BF16), ffn2_w_down[l].astype(BF16),
                 row(ln3_g[l]), row(ln3_b[l]))
    return h.reshape(bsz, seq, D_MODEL)
```

```python
import functools

import jax
import jax.numpy as jnp
from jax import lax
from jax.experimental import pallas as pl
from jax.experimental.pallas import tpu as pltpu

D_MODEL = 2048
DEPTH = 1
CHUNK = 64
GMLP_BLOCK = 128
A_WIDTH = D_MODEL // 2
A_GROUPS = 8
A_HEAD = A_WIDTH // A_GROUPS
B_WIDTH = D_MODEL // 2
CONV_WIDTH = 31
D_FF = ((8 * D_MODEL // 3 + 255) // 256) * 256
LN_EPS = 1e-5
ALPHA = (2.0 * DEPTH) ** 0.25

F32 = jnp.float32
BF16 = jnp.bfloat16

V7X_LANES = 128
V7X_MXU_COLS = 256
V7X_VMEM_LIMIT_BYTES = 60000 * 1024

FFN_ROWS = 1024
FFN_COLS = 512
MIX_ROWS = 512
LN_ROWS = 64
EPILOGUE_ROWS = 256
CONV_ROWS = 64
CONV_HALO = 32
CONV_LANE_CHUNKS = B_WIDTH // V7X_LANES


def _layer_norm(x, g, b):
    mu = jnp.mean(x, axis=-1, keepdims=True)
    xc = x - mu
    var = jnp.mean(xc * xc, axis=-1, keepdims=True)
    return xc * lax.rsqrt(var + LN_EPS) * g + b


def _layer_norm_rows(src_ref, dst_ref, g, b, post=None):
    rows = src_ref.shape[0]

    def body(r, carry):
        sl = pl.ds(pl.multiple_of(r * LN_ROWS, LN_ROWS), LN_ROWS)
        y = _layer_norm(src_ref[sl, :].astype(F32), g, b)
        if post is not None:
            y = post(y)
        dst_ref[sl, :] = y.astype(dst_ref.dtype)
        return carry

    lax.fori_loop(0, rows // LN_ROWS, body, 0, unroll=2)


def _layer_norm_rows_unrolled(src_ref, dst_ref, g, b, post=None):
    for r in range(src_ref.shape[0] // LN_ROWS):
        sl = slice(r * LN_ROWS, (r + 1) * LN_ROWS)
        y = _layer_norm(src_ref[sl, :].astype(F32), g, b)
        if post is not None:
            y = post(y)
        dst_ref[sl, :] = y.astype(dst_ref.dtype)


def _gelu(x):
    return jax.nn.gelu(x, approximate=True)


def _acc_layer_norm_rows(o_ref, lhs, w_ref, g, b):
    for r in range(o_ref.shape[0] // EPILOGUE_ROWS):
        rows = slice(r * EPILOGUE_ROWS, (r + 1) * EPILOGUE_ROWS)
        y = o_ref[rows, :] + jnp.dot(lhs[rows], w_ref[...], preferred_element_type=F32)
        for q in range(EPILOGUE_ROWS // LN_ROWS):
            sub = slice(q * LN_ROWS, (q + 1) * LN_ROWS)
            o_ref[r * EPILOGUE_ROWS + q * LN_ROWS:r * EPILOGUE_ROWS + (q + 1) * LN_ROWS, :] = (
                _layer_norm(y[sub], g, b))


def _ffn_kernel(n_side, x_ref, wg_ref, wu_ref, wd_ref, g_ref, b_ref, *refs):
    side_in, o_ref, side_out, xb_ref = refs[:n_side], refs[n_side], refs[n_side + 1:2 * n_side + 1], refs[-1]
    j = pl.program_id(1)
    last = pl.num_programs(1) - 1

    def step(is_first, is_last):
        for src_ref, dst_ref in zip(side_in, side_out):
            dst_ref[...] = src_ref[...].astype(BF16)
        if is_first:
            xb_ref[...] = x_ref[...].astype(BF16)
        xb = xb_ref[...]
        hg = jnp.dot(xb, wg_ref[...], preferred_element_type=F32)
        hu = jnp.dot(xb, wu_ref[...], preferred_element_type=F32)
        act = (0.5 * (jax.nn.silu(hg) * hu)).astype(BF16)
        if is_last:
            _acc_layer_norm_rows(o_ref, act, wd_ref, g_ref[...], b_ref[...])
        else:
            d = jnp.dot(act, wd_ref[...], preferred_element_type=F32)
            if is_first:
                o_ref[...] = ALPHA * x_ref[...] + d
            else:
                o_ref[...] += d

    pl.when(j == 0)(functools.partial(step, True, False))
    pl.when(jnp.logical_and(j > 0, j < last))(functools.partial(step, False, False))
    pl.when(j == last)(functools.partial(step, False, True))


def _ffn(x, w_gu, w_down, g, b, side=()):
    m = x.shape[0]
    nj = D_FF // FFN_COLS
    side_specs = [pl.BlockSpec(blk, imap) for _, blk, imap in side]
    outs = pl.pallas_call(
        functools.partial(_ffn_kernel, len(side)),
        grid=(m // FFN_ROWS, nj),
        in_specs=[
            pl.BlockSpec((FFN_ROWS, D_MODEL), lambda i, j: (i, 0)),
            pl.BlockSpec((D_MODEL, FFN_COLS), lambda i, j: (0, j)),
            pl.BlockSpec((D_MODEL, FFN_COLS), lambda i, j: (0, j + nj)),
            pl.BlockSpec((FFN_COLS, D_MODEL), lambda i, j: (j, 0)),
            pl.BlockSpec((1, D_MODEL), lambda i, j: (0, 0)),
            pl.BlockSpec((1, D_MODEL), lambda i, j: (0, 0)),
        ] + side_specs,
        out_specs=[pl.BlockSpec((FFN_ROWS, D_MODEL), lambda i, j: (i, 0))] + side_specs,
        out_shape=[jax.ShapeDtypeStruct((m, D_MODEL), F32)]
        + [jax.ShapeDtypeStruct(w.shape, BF16) for w, _, _ in side],
        scratch_shapes=[pltpu.VMEM((FFN_ROWS, D_MODEL), BF16)],
        compiler_params=pltpu.CompilerParams(
            dimension_semantics=("parallel", "arbitrary"),
            vmem_limit_bytes=V7X_VMEM_LIMIT_BYTES),
        name="ffn_swiglu_ln",
    )(x, w_gu, w_gu, w_down, g, b, *[w for w, _, _ in side])
    return outs[0], outs[1:]


def _conv_lane_chunk(zbuf_ref, wdw_ref, bdw_ref, out_ref, c):
    tm = out_ref.shape[0]
    cs = slice(c * V7X_LANES, (c + 1) * V7X_LANES)
    first_off = CONV_HALO - (CONV_WIDTH - 1)
    bias = bdw_ref[:, cs]
    for rb in range(tm // CONV_ROWS):
        base = rb * CONV_ROWS
        acc = None
        for k in range(CONV_WIDTH):
            rows = pl.ds(base + first_off + k, CONV_ROWS, stride=1)
            term = wdw_ref[k:k + 1, cs] * zbuf_ref[c, rows, :]
            acc = term if acc is None else acc + term
        out_ref[base:base + CONV_ROWS, cs] = acc + bias


def _mixer_branches_kernel(tiles_per_seq, x_ref, w_ref, bin_ref,
                           sg_ref, sb_ref, ws_ref, bs_ref, wdw_ref, bdw_ref, cg_ref, cb_ref,
                           a_out, b_out, xb_ref, u_ref, v_ref, vb_ref, zbuf_ref, conv_ref):
    i = pl.program_id(0)
    tm = x_ref.shape[0]
    lanes_per_sub = V7X_MXU_COLS // V7X_LANES
    u_col, v_col, a_col, g_col = 0, A_WIDTH, 2 * A_WIDTH, 2 * A_WIDTH + B_WIDTH

    def project(col0):
        cs = slice(col0, col0 + V7X_MXU_COLS)
        return jnp.dot(xb_ref[...], w_ref[:, cs], preferred_element_type=F32) + bin_ref[:, cs]

    def project_gelu(dst_ref, col0, n):
        p = project(col0 + n * V7X_MXU_COLS)
        for r in range(tm // LN_ROWS):
            rs = slice(r * LN_ROWS, (r + 1) * LN_ROWS)
            dst_ref[rs, n * V7X_MXU_COLS:(n + 1) * V7X_MXU_COLS] = _gelu(p[rs])

    def project_glu(n):
        seq_start = (i % tiles_per_seq) == 0
        a = project(a_col + n * V7X_MXU_COLS)
        g = project(g_col + n * V7X_MXU_COLS)
        for cc in range(lanes_per_sub):
            c = n * lanes_per_sub + cc
            ls = slice(cc * V7X_LANES, (cc + 1) * V7X_LANES)
            tail = zbuf_ref[c, tm:tm + CONV_HALO, :]
            zbuf_ref[c, 0:CONV_HALO, :] = jnp.where(seq_start, 0.0, tail)
            for r in range(tm // LN_ROWS):
                rs = slice(r * LN_ROWS, (r + 1) * LN_ROWS)
                zs = slice(CONV_HALO + r * LN_ROWS, CONV_HALO + (r + 1) * LN_ROWS)
                zbuf_ref[c, zs, :] = a[rs, ls] * jax.nn.sigmoid(g[rs, ls])

    def conv(c):
        _conv_lane_chunk(zbuf_ref, wdw_ref, bdw_ref, conv_ref, c)

    @pl.when(i == 0)
    def _():
        zbuf_ref[...] = jnp.zeros_like(zbuf_ref)

    xb_ref[...] = x_ref[...].astype(BF16)
    project_glu(0)
    project_glu(1)
    conv(0)
    conv(1)
    project_glu(2)
    conv(2)
    conv(3)
    project_glu(3)
    conv(4)
    project_gelu(u_ref, u_col, 0)
    project_gelu(u_ref, u_col, 1)
    conv(5)
    project_gelu(u_ref, u_col, 2)
    conv(6)
    project_gelu(u_ref, u_col, 3)
    project_gelu(v_ref, v_col, 0)
    project_gelu(v_ref, v_col, 1)
    conv(7)
    project_gelu(v_ref, v_col, 2)
    _layer_norm_rows_unrolled(conv_ref, b_out, cg_ref[...], cb_ref[...], post=jax.nn.silu)
    project_gelu(v_ref, v_col, 3)
    _layer_norm_rows_unrolled(v_ref, vb_ref, sg_ref[...], sb_ref[...])
    row = lax.broadcasted_iota(jnp.int32, (GMLP_BLOCK, GMLP_BLOCK), 0)
    col = lax.broadcasted_iota(jnp.int32, (GMLP_BLOCK, GMLP_BLOCK), 1)
    keep = (col // CHUNK) <= (row // CHUNK)
    for h in range(A_GROUPS):
        wm = jnp.where(keep, ws_ref[h], 0.0).astype(BF16)
        bias = bs_ref[h]
        cs = slice(h * A_HEAD, (h + 1) * A_HEAD)
        for blk in range(tm // GMLP_BLOCK):
            rs = slice(blk * GMLP_BLOCK, (blk + 1) * GMLP_BLOCK)
            s = jnp.dot(wm, vb_ref[rs, cs], preferred_element_type=F32) + bias
            a_out[rs, cs] = (u_ref[rs, cs] * s).astype(a_out.dtype)


def _mixer_branches(x, w_in, b_in, sgu_g, sgu_b, w_s, b_s, w_dw, b_dw, conv_g, conv_b, seq):
    m = x.shape[0]
    tm = MIX_ROWS
    branch_cols = 2 * A_WIDTH + 2 * B_WIDTH
    full = lambda shape: pl.BlockSpec(shape, lambda i: (0,) * len(shape), pipeline_mode=pl.Buffered(1))
    kernel = functools.partial(_mixer_branches_kernel, seq // tm)
    return pl.pallas_call(
        kernel,
        grid=(m // tm,),
        in_specs=[
            pl.BlockSpec((tm, D_MODEL), lambda i: (i, 0)),
            full((D_MODEL, branch_cols)),
            full((1, branch_cols)),
            full((1, A_WIDTH)), full((1, A_WIDTH)),
            full((A_GROUPS, GMLP_BLOCK, GMLP_BLOCK)),
            full((A_GROUPS, GMLP_BLOCK, A_HEAD)),
            full((CONV_WIDTH, B_WIDTH)), full((1, B_WIDTH)),
            full((1, B_WIDTH)), full((1, B_WIDTH)),
        ],
        out_specs=[
            pl.BlockSpec((tm, A_WIDTH), lambda i: (i, 0)),
            pl.BlockSpec((tm, B_WIDTH), lambda i: (i, 0)),
        ],
        out_shape=[
            jax.ShapeDtypeStruct((m, A_WIDTH), BF16),
            jax.ShapeDtypeStruct((m, B_WIDTH), BF16),
        ],
        scratch_shapes=[
            pltpu.VMEM((tm, D_MODEL), BF16),
            pltpu.VMEM((tm, A_WIDTH), F32),
            pltpu.VMEM((tm, A_WIDTH), F32),
            pltpu.VMEM((tm, A_WIDTH), BF16),
            pltpu.VMEM((CONV_LANE_CHUNKS, tm + CONV_HALO, V7X_LANES), F32),
            pltpu.VMEM((tm, B_WIDTH), F32),
        ],
        compiler_params=pltpu.CompilerParams(
            dimension_semantics=("arbitrary",),
            vmem_limit_bytes=V7X_VMEM_LIMIT_BYTES),
        name="mixer_branches",
    )(x, w_in, b_in, sgu_g, sgu_b, w_s, b_s, w_dw, b_dw, conv_g, conv_b)


def _mixer_merge_kernel(x_ref, a_ref, b_ref, wa_ref, wb_ref, wg_ref, bg_ref, wo_ref, g_ref, bt_ref,
                        o_ref, xb_ref, mix_ref):
    tm = x_ref.shape[0]
    xb_ref[...] = x_ref[...].astype(BF16)
    for c in range(D_MODEL // V7X_MXU_COLS):
        cs = slice(c * V7X_MXU_COLS, (c + 1) * V7X_MXU_COLS)
        gs = slice(D_MODEL + c * V7X_MXU_COLS, D_MODEL + (c + 1) * V7X_MXU_COLS)
        ya = jnp.dot(a_ref[...], wa_ref[:, cs], preferred_element_type=F32)
        yb = jnp.dot(b_ref[...], wb_ref[:, cs], preferred_element_type=F32)
        la = jnp.dot(xb_ref[...], wg_ref[:, cs], preferred_element_type=F32) + bg_ref[:, cs]
        lb = jnp.dot(xb_ref[...], wg_ref[:, gs], preferred_element_type=F32) + bg_ref[:, gs]
        for r in range(tm // LN_ROWS):
            rs = slice(r * LN_ROWS, (r + 1) * LN_ROWS)
            mix = jax.nn.sigmoid(la[rs]) * ya[rs] + jax.nn.sigmoid(lb[rs]) * yb[rs]
            mix_ref[rs, cs] = mix.astype(BF16)
    g = g_ref[...]
    bt = bt_ref[...]
    for r in range(tm // EPILOGUE_ROWS):
        rows = slice(r * EPILOGUE_ROWS, (r + 1) * EPILOGUE_ROWS)
        y = ALPHA * x_ref[rows, :] + jnp.dot(mix_ref[rows, :], wo_ref[...], preferred_element_type=F32)
        for q in range(EPILOGUE_ROWS // LN_ROWS):
            sub = slice(q * LN_ROWS, (q + 1) * LN_ROWS)
            o_ref[r * EPILOGUE_ROWS + q * LN_ROWS:r * EPILOGUE_ROWS + (q + 1) * LN_ROWS, :] = (
                _layer_norm(y[sub], g, bt))


def _mixer_merge(x, a, b, w_a, w_b, w_in, b_in, w_out, g, bt):
    m = x.shape[0]
    tm = MIX_ROWS
    full = lambda shape: pl.BlockSpec(shape, lambda i: (0,) * len(shape), pipeline_mode=pl.Buffered(1))
    gates = lambda shape: pl.BlockSpec(shape, lambda i: (0, 1), pipeline_mode=pl.Buffered(1))
    return pl.pallas_call(
        _mixer_merge_kernel,
        grid=(m // tm,),
        in_specs=[
            pl.BlockSpec((tm, D_MODEL), lambda i: (i, 0)),
            pl.BlockSpec((tm, A_WIDTH), lambda i: (i, 0)),
            pl.BlockSpec((tm, B_WIDTH), lambda i: (i, 0)),
            full((A_WIDTH, D_MODEL)),
            full((B_WIDTH, D_MODEL)),
            gates((D_MODEL, 2 * D_MODEL)),
            gates((1, 2 * D_MODEL)),
            full((D_MODEL, D_MODEL)),
            full((1, D_MODEL)),
            full((1, D_MODEL)),
        ],
        out_specs=pl.BlockSpec((tm, D_MODEL), lambda i: (i, 0)),
        out_shape=jax.ShapeDtypeStruct((m, D_MODEL), F32),
        scratch_shapes=[
            pltpu.VMEM((tm, D_MODEL), BF16),
            pltpu.VMEM((tm, D_MODEL), BF16),
        ],
        compiler_params=pltpu.CompilerParams(
            dimension_semantics=("parallel",),
            vmem_limit_bytes=V7X_VMEM_LIMIT_BYTES),
        name="mixer_merge",
    )(x, a, b, w_a, w_b, w_in, b_in, w_out, g, bt)


def kernel(x, ffn1_w_gu, ffn1_w_down, ln1_g, ln1_b, w_in, b_in, sgu_ln_g, sgu_ln_b, sgu_w_s, sgu_b_s, w_a_proj, conv_w_dw, conv_b_dw, conv_ln_g, conv_ln_b, w_b_proj, w_out, ln2_g, ln2_b, ffn2_w_gu, ffn2_w_down, ln3_g, ln3_b):
    bsz, seq, _ = x.shape
    h = x.reshape(bsz * seq, D_MODEL)
    row = lambda v: v.reshape(1, -1)

    n_tiles = (bsz * seq) // FFN_ROWS
    n_steps = D_FF // FFN_COLS
    side_rows = D_MODEL // n_tiles
    side_cols = 2 * D_FF // n_steps
    in_blocks = w_in.shape[-1] // side_cols
    for l in range(DEPTH):
        side = (
            (ffn2_w_gu[l], (side_rows, side_cols), lambda i, j: (i, j)),
            (ffn2_w_down[l], (D_FF // (n_tiles * n_steps), D_MODEL), lambda i, j: (i * n_steps + j, 0)),
            (w_in[l], (side_rows, side_cols), lambda i, j: (i, jnp.minimum(j, in_blocks - 1))),
        )
        h, (w_gu2, w_down2, w_in_b) = _ffn(h, ffn1_w_gu[l].astype(BF16), ffn1_w_down[l].astype(BF16),
                                           row(ln1_g[l]), row(ln1_b[l]), side)
        b_in_r = row(b_in[l])
        b_s = jnp.broadcast_to(sgu_b_s[l][:, :, None], (A_GROUPS, GMLP_BLOCK, A_HEAD))
        a, b = _mixer_branches(h, w_in_b, b_in_r, row(sgu_ln_g[l]), row(sgu_ln_b[l]),
                               sgu_w_s[l], b_s, conv_w_dw[l], row(conv_b_dw[l]),
                               row(conv_ln_g[l]), row(conv_ln_b[l]), seq)
        h = _mixer_merge(h, a, b, w_a_proj[l].astype(BF16), w_b_proj[l].astype(BF16),
                         w_in_b, b_in_r, w_out[l].astype(BF16), row(ln2_g[l]), row(ln2_b[l]))
        h, _ = _ffn(h, w_gu2, w_down2, row(ln3_g[l]), row(ln3_b[l]))
    return h.reshape(bsz, seq, D_MODEL)
```

```python
import functools

import jax
import jax.numpy as jnp
from jax import lax
from jax.experimental import pallas as pl
from jax.experimental.pallas import tpu as pltpu

D_MODEL = 2048
DEPTH = 1
CHUNK = 64
GMLP_BLOCK = 128
A_WIDTH = D_MODEL // 2
A_GROUPS = 8
A_HEAD = A_WIDTH // A_GROUPS
B_WIDTH = D_MODEL // 2
CONV_WIDTH = 31
D_FF = ((8 * D_MODEL // 3 + 255) // 256) * 256
LN_EPS = 1e-5
ALPHA = (2.0 * DEPTH) ** 0.25

F32 = jnp.float32
BF16 = jnp.bfloat16

V7X_LANES = 128
V7X_MXU_COLS = 256
V7X_VMEM_LIMIT_BYTES = 60000 * 1024

FFN_ROWS = 1024
FFN_COLS = 512
MIX_ROWS = 512
LN_ROWS = 64
EPILOGUE_ROWS = 256
CONV_ROWS = 64
CONV_HALO = 32
CONV_LANE_CHUNKS = B_WIDTH // V7X_LANES


def _layer_norm(x, g, b):
    mu = jnp.mean(x, axis=-1, keepdims=True)
    xc = x - mu
    var = jnp.mean(xc * xc, axis=-1, keepdims=True)
    return xc * lax.rsqrt(var + LN_EPS) * g + b


def _layer_norm_rows_unrolled(src_ref, dst_ref, g, b, post=None):
    for r in range(src_ref.shape[0] // LN_ROWS):
        sl = slice(r * LN_ROWS, (r + 1) * LN_ROWS)
        y = _layer_norm(src_ref[sl, :].astype(F32), g, b)
        if post is not None:
            y = post(y)
        dst_ref[sl, :] = y.astype(dst_ref.dtype)


def _gelu(x):
    return jax.nn.gelu(x, approximate=True)


def _acc_layer_norm_rows(o_ref, lhs, w_ref, g, b):
    for r in range(o_ref.shape[0] // EPILOGUE_ROWS):
        rows = slice(r * EPILOGUE_ROWS, (r + 1) * EPILOGUE_ROWS)
        y = o_ref[rows, :] + jnp.dot(lhs[rows], w_ref[...], preferred_element_type=F32)
        for q in range(EPILOGUE_ROWS // LN_ROWS):
            sub = slice(q * LN_ROWS, (q + 1) * LN_ROWS)
            o_ref[r * EPILOGUE_ROWS + q * LN_ROWS:r * EPILOGUE_ROWS + (q + 1) * LN_ROWS, :] = (
                _layer_norm(y[sub], g, b))


def _ffn_kernel(n_side, x_ref, wg_ref, wu_ref, wd_ref, g_ref, b_ref, *refs):
    side_in, o_ref, side_out, xb_ref = refs[:n_side], refs[n_side], refs[n_side + 1:2 * n_side + 1], refs[-1]
    j = pl.program_id(1)
    last = pl.num_programs(1) - 1

    def step(is_first, is_last):
        for src_ref, dst_ref in zip(side_in, side_out):
            dst_ref[...] = src_ref[...].astype(BF16)
        if is_first:
            xb_ref[...] = x_ref[...].astype(BF16)
        xb = xb_ref[...]
        hg = jnp.dot(xb, wg_ref[...], preferred_element_type=F32)
        hu = jnp.dot(xb, wu_ref[...], preferred_element_type=F32)
        act = (0.5 * (jax.nn.silu(hg) * hu)).astype(BF16)
        if is_last:
            _acc_layer_norm_rows(o_ref, act, wd_ref, g_ref[...], b_ref[...])
        else:
            d = jnp.dot(act, wd_ref[...], preferred_element_type=F32)
            if is_first:
                o_ref[...] = ALPHA * x_ref[...] + d
            else:
                o_ref[...] += d

    pl.when(j == 0)(functools.partial(step, True, False))
    pl.when(jnp.logical_and(j > 0, j < last))(functools.partial(step, False, False))
    pl.when(j == last)(functools.partial(step, False, True))


def _ffn(x, w_gu, w_down, g, b, side=()):
    m = x.shape[0]
    nj = D_FF // FFN_COLS
    side_specs = [pl.BlockSpec(blk, imap) for _, blk, imap in side]
    outs = pl.pallas_call(
        functools.partial(_ffn_kernel, len(side)),
        grid=(m // FFN_ROWS, nj),
        in_specs=[
            pl.BlockSpec((FFN_ROWS, D_MODEL), lambda i, j: (i, 0)),
            pl.BlockSpec((D_MODEL, FFN_COLS), lambda i, j: (0, j)),
            pl.BlockSpec((D_MODEL, FFN_COLS), lambda i, j: (0, j + nj)),
            pl.BlockSpec((FFN_COLS, D_MODEL), lambda i, j: (j, 0)),
            pl.BlockSpec((1, D_MODEL), lambda i, j: (0, 0)),
            pl.BlockSpec((1, D_MODEL), lambda i, j: (0, 0)),
        ] + side_specs,
        out_specs=[pl.BlockSpec((FFN_ROWS, D_MODEL), lambda i, j: (i, 0))] + side_specs,
        out_shape=[jax.ShapeDtypeStruct((m, D_MODEL), F32)]
        + [jax.ShapeDtypeStruct(w.shape, BF16) for w, _, _ in side],
        scratch_shapes=[pltpu.VMEM((FFN_ROWS, D_MODEL), BF16)],
        compiler_params=pltpu.CompilerParams(
            dimension_semantics=("parallel", "arbitrary"),
            vmem_limit_bytes=V7X_VMEM_LIMIT_BYTES),
        name="ffn_swiglu_ln",
    )(x, w_gu, w_gu, w_down, g, b, *[w for w, _, _ in side])
    return outs[0], outs[1:]


def _conv_lane_chunk(zbuf_ref, wdw_ref, bdw_ref, out_ref, c):
    tm = out_ref.shape[0]
    cs = slice(c * V7X_LANES, (c + 1) * V7X_LANES)
    first_off = CONV_HALO - (CONV_WIDTH - 1)
    bias = bdw_ref[:, cs]
    for rb in range(tm // CONV_ROWS):
        base = rb * CONV_ROWS
        acc = None
        for k in range(CONV_WIDTH):
            rows = pl.ds(base + first_off + k, CONV_ROWS, stride=1)
            term = wdw_ref[k:k + 1, cs] * zbuf_ref[c, rows, :]
            acc = term if acc is None else acc + term
        out_ref[base:base + CONV_ROWS, cs] = acc + bias


def _mixer_branches_kernel(tiles_per_seq, x_ref, w_ref, bin_ref,
                           sg_ref, sb_ref, ws_ref, bs_ref, wdw_ref, bdw_ref, cg_ref, cb_ref,
                           a_out, b_out, xb_ref, u_ref, v_ref, vb_ref, zbuf_ref, conv_ref):
    i = pl.program_id(0)
    tm = x_ref.shape[0]
    lanes_per_sub = V7X_MXU_COLS // V7X_LANES
    u_col, v_col, a_col, g_col = 0, A_WIDTH, 2 * A_WIDTH, 2 * A_WIDTH + B_WIDTH

    def project(col0):
        cs = slice(col0, col0 + V7X_MXU_COLS)
        return jnp.dot(xb_ref[...], w_ref[:, cs], preferred_element_type=F32) + bin_ref[:, cs]

    def project_gelu(dst_ref, col0, n):
        p = project(col0 + n * V7X_MXU_COLS)
        for r in range(tm // LN_ROWS):
            rs = slice(r * LN_ROWS, (r + 1) * LN_ROWS)
            dst_ref[rs, n * V7X_MXU_COLS:(n + 1) * V7X_MXU_COLS] = _gelu(p[rs])

    def project_glu(n):
        seq_start = (i % tiles_per_seq) == 0
        a = project(a_col + n * V7X_MXU_COLS)
        g = project(g_col + n * V7X_MXU_COLS)
        for cc in range(lanes_per_sub):
            c = n * lanes_per_sub + cc
            ls = slice(cc * V7X_LANES, (cc + 1) * V7X_LANES)
            tail = zbuf_ref[c, tm:tm + CONV_HALO, :]
            zbuf_ref[c, 0:CONV_HALO, :] = jnp.where(seq_start, 0.0, tail)
            for r in range(tm // LN_ROWS):
                rs = slice(r * LN_ROWS, (r + 1) * LN_ROWS)
                zs = slice(CONV_HALO + r * LN_ROWS, CONV_HALO + (r + 1) * LN_ROWS)
                zbuf_ref[c, zs, :] = a[rs, ls] * jax.nn.sigmoid(g[rs, ls])

    def conv(c):
        _conv_lane_chunk(zbuf_ref, wdw_ref, bdw_ref, conv_ref, c)

    @pl.when(i == 0)
    def _():
        zbuf_ref[...] = jnp.zeros_like(zbuf_ref)

    xb_ref[...] = x_ref[...].astype(BF16)
    project_glu(0)
    project_glu(1)
    conv(0)
    conv(1)
    project_glu(2)
    conv(2)
    conv(3)
    project_glu(3)
    conv(4)
    project_gelu(u_ref, u_col, 0)
    project_gelu(u_ref, u_col, 1)
    conv(5)
    project_gelu(u_ref, u_col, 2)
    conv(6)
    project_gelu(u_ref, u_col, 3)
    project_gelu(v_ref, v_col, 0)
    project_gelu(v_ref, v_col, 1)
    conv(7)
    project_gelu(v_ref, v_col, 2)
    _layer_norm_rows_unrolled(conv_ref, b_out, cg_ref[...], cb_ref[...], post=jax.nn.silu)
    project_gelu(v_ref, v_col, 3)
    _layer_norm_rows_unrolled(v_ref, vb_ref, sg_ref[...], sb_ref[...])
    row = lax.broadcasted_iota(jnp.int32, (GMLP_BLOCK, GMLP_BLOCK), 0)
    col = lax.broadcasted_iota(jnp.int32, (GMLP_BLOCK, GMLP_BLOCK), 1)
    keep = (col // CHUNK) <= (row // CHUNK)
    for h in range(A_GROUPS):
        wm = jnp.where(keep, ws_ref[h], 0.0).astype(BF16)
        bias = bs_ref[h]
        cs = slice(h * A_HEAD, (h + 1) * A_HEAD)
        for blk in range(tm // GMLP_BLOCK):
            rs = slice(blk * GMLP_BLOCK, (blk + 1) * GMLP_BLOCK)
            s = jnp.dot(wm, vb_ref[rs, cs], preferred_element_type=F32) + bias
            a_out[rs, cs] = (u_ref[rs, cs] * s).astype(a_out.dtype)


def _mixer_branches(x, w_in, b_in, sgu_g, sgu_b, w_s, b_s, w_dw, b_dw, conv_g, conv_b, seq):
    m = x.shape[0]
    tm = MIX_ROWS
    branch_cols = 2 * A_WIDTH + 2 * B_WIDTH
    full = lambda shape: pl.BlockSpec(shape, lambda i: (0,) * len(shape), pipeline_mode=pl.Buffered(1))
    kernel = functools.partial(_mixer_branches_kernel, seq // tm)
    return pl.pallas_call(
        kernel,
        grid=(m // tm,),
        in_specs=[
            pl.BlockSpec((tm, D_MODEL), lambda i: (i, 0)),
            full((D_MODEL, branch_cols)),
            full((1, branch_cols)),
            full((1, A_WIDTH)), full((1, A_WIDTH)),
            full((A_GROUPS, GMLP_BLOCK, GMLP_BLOCK)),
            full((A_GROUPS, GMLP_BLOCK, A_HEAD)),
            full((CONV_WIDTH, B_WIDTH)), full((1, B_WIDTH)),
            full((1, B_WIDTH)), full((1, B_WIDTH)),
        ],
        out_specs=[
            pl.BlockSpec((tm, A_WIDTH), lambda i: (i, 0)),
            pl.BlockSpec((tm, B_WIDTH), lambda i: (i, 0)),
        ],
        out_shape=[
            jax.ShapeDtypeStruct((m, A_WIDTH), BF16),
            jax.ShapeDtypeStruct((m, B_WIDTH), BF16),
        ],
        scratch_shapes=[
            pltpu.VMEM((tm, D_MODEL), BF16),
            pltpu.VMEM((tm, A_WIDTH), F32),
            pltpu.VMEM((tm, A_WIDTH), F32),
            pltpu.VMEM((tm, A_WIDTH), BF16),
            pltpu.VMEM((CONV_LANE_CHUNKS, tm + CONV_HALO, V7X_LANES), F32),
            pltpu.VMEM((tm, B_WIDTH), F32),
        ],
        compiler_params=pltpu.CompilerParams(
            dimension_semantics=("arbitrary",),
            vmem_limit_bytes=V7X_VMEM_LIMIT_BYTES),
        name="mixer_branches",
    )(x, w_in, b_in, sgu_g, sgu_b, w_s, b_s, w_dw, b_dw, conv_g, conv_b)


def _mixer_merge_kernel(x_ref, a_ref, b_ref, wa_ref, wb_ref, wg_ref, bg_ref, wo_ref, g_ref, bt_ref,
                        o_ref, xb_ref, mix_ref):
    tm = x_ref.shape[0]
    xb_ref[...] = x_ref[...].astype(BF16)
    for c in range(D_MODEL // V7X_MXU_COLS):
        cs = slice(c * V7X_MXU_COLS, (c + 1) * V7X_MXU_COLS)
        gs = slice(D_MODEL + c * V7X_MXU_COLS, D_MODEL + (c + 1) * V7X_MXU_COLS)
        ya = jnp.dot(a_ref[...], wa_ref[:, cs], preferred_element_type=F32)
        yb = jnp.dot(b_ref[...], wb_ref[:, cs], preferred_element_type=F32)
        la = jnp.dot(xb_ref[...], wg_ref[:, cs], preferred_element_type=F32) + bg_ref[:, cs]
        lb = jnp.dot(xb_ref[...], wg_ref[:, gs], preferred_element_type=F32) + bg_ref[:, gs]
        for r in range(tm // LN_ROWS):
            rs = slice(r * LN_ROWS, (r + 1) * LN_ROWS)
            mix = jax.nn.sigmoid(la[rs]) * ya[rs] + jax.nn.sigmoid(lb[rs]) * yb[rs]
            mix_ref[rs, cs] = mix.astype(BF16)
    g = g_ref[...]
    bt = bt_ref[...]
    for r in range(tm // EPILOGUE_ROWS):
        rows = slice(r * EPILOGUE_ROWS, (r + 1) * EPILOGUE_ROWS)
        y = ALPHA * x_ref[rows, :] + jnp.dot(mix_ref[rows, :], wo_ref[...], preferred_element_type=F32)
        for q in range(EPILOGUE_ROWS // LN_ROWS):
            sub = slice(q * LN_ROWS, (q + 1) * LN_ROWS)
            o_ref[r * EPILOGUE_ROWS + q * LN_ROWS:r * EPILOGUE_ROWS + (q + 1) * LN_ROWS, :] = (
                _layer_norm(y[sub], g, bt))


def _mixer_merge(x, a, b, w_a, w_b, w_in, b_in, w_out, g, bt):
    m = x.shape[0]
    tm = MIX_ROWS
    full = lambda shape: pl.BlockSpec(shape, lambda i: (0,) * len(shape), pipeline_mode=pl.Buffered(1))
    gates = lambda shape: pl.BlockSpec(shape, lambda i: (0, 1), pipeline_mode=pl.Buffered(1))
    return pl.pallas_call(
        _mixer_merge_kernel,
        grid=(m // tm,),
        in_specs=[
            pl.BlockSpec((tm, D_MODEL), lambda i: (i, 0)),
            pl.BlockSpec((tm, A_WIDTH), lambda i: (i, 0)),
            pl.BlockSpec((tm, B_WIDTH), lambda i: (i, 0)),
            full((A_WIDTH, D_MODEL)),
            full((B_WIDTH, D_MODEL)),
            gates((D_MODEL, 2 * D_MODEL)),
            gates((1, 2 * D_MODEL)),
            full((D_MODEL, D_MODEL)),
            full((1, D_MODEL)),
            full((1, D_MODEL)),
        ],
        out_specs=pl.BlockSpec((tm, D_MODEL), lambda i: (i, 0)),
        out_shape=jax.ShapeDtypeStruct((m, D_MODEL), F32),
        scratch_shapes=[
            pltpu.VMEM((tm, D_MODEL), BF16),
            pltpu.VMEM((tm, D_MODEL), BF16),
        ],
        compiler_params=pltpu.CompilerParams(
            dimension_semantics=("parallel",),
            vmem_limit_bytes=V7X_VMEM_LIMIT_BYTES),
        name="mixer_merge",
    )(x, a, b, w_a, w_b, w_in, b_in, w_out, g, bt)


def kernel(x, ffn1_w_gu, ffn1_w_down, ln1_g, ln1_b, w_in, b_in, sgu_ln_g, sgu_ln_b, sgu_w_s, sgu_b_s, w_a_proj, conv_w_dw, conv_b_dw, conv_ln_g, conv_ln_b, w_b_proj, w_out, ln2_g, ln2_b, ffn2_w_gu, ffn2_w_down, ln3_g, ln3_b):
    bsz, seq, _ = x.shape
    h = x.reshape(bsz * seq, D_MODEL)
    row = lambda v: v.reshape(1, -1)

    n_tiles = (bsz * seq) // FFN_ROWS
    n_steps = D_FF // FFN_COLS
    side_rows = D_MODEL // n_tiles
    side_cols = 2 * D_FF // n_steps
    in_blocks = w_in.shape[-1] // side_cols
    for l in range(DEPTH):
        side = (
            (ffn2_w_gu[l], (side_rows, side_cols), lambda i, j: (i, j)),
            (ffn2_w_down[l], (D_FF // (n_tiles * n_steps), D_MODEL), lambda i, j: (i * n_steps + j, 0)),
            (w_in[l], (side_rows, side_cols), lambda i, j: (i, jnp.minimum(j, in_blocks - 1))),
        )
        h, (w_gu2, w_down2, w_in_b) = _ffn(h, ffn1_w_gu[l].astype(BF16), ffn1_w_down[l].astype(BF16),
                                           row(ln1_g[l]), row(ln1_b[l]), side)
        b_in_r = row(b_in[l])
        b_s = jnp.broadcast_to(sgu_b_s[l][:, :, None], (A_GROUPS, GMLP_BLOCK, A_HEAD))
        a, b = _mixer_branches(h, w_in_b, b_in_r, row(sgu_ln_g[l]), row(sgu_ln_b[l]),
                               sgu_w_s[l], b_s, conv_w_dw[l], row(conv_b_dw[l]),
                               row(conv_ln_g[l]), row(conv_ln_b[l]), seq)
        h = _mixer_merge(h, a, b, w_a_proj[l].astype(BF16), w_b_proj[l].astype(BF16),
                         w_in_b, b_in_r, w_out[l].astype(BF16), row(ln2_g[l]), row(ln2_b[l]))
        h, _ = _ffn(h, w_gu2, w_down2, row(ln3_g[l]), row(ln3_b[l]))
    return h.reshape(bsz, seq, D_MODEL)
```

```python
import functools

import jax
import jax.numpy as jnp
from jax import lax
from jax.experimental import pallas as pl
from jax.experimental.pallas import tpu as pltpu

D_MODEL = 2048
DEPTH = 1
CHUNK = 64
GMLP_BLOCK = 128
A_WIDTH = D_MODEL // 2
A_GROUPS = 8
A_HEAD = A_WIDTH // A_GROUPS
B_WIDTH = D_MODEL // 2
CONV_WIDTH = 31
D_FF = ((8 * D_MODEL // 3 + 255) // 256) * 256
LN_EPS = 1e-5
ALPHA = (2.0 * DEPTH) ** 0.25

F32 = jnp.float32
BF16 = jnp.bfloat16

V7X_LANES = 128
V7X_MXU_COLS = 256
V7X_VMEM_LIMIT_BYTES = 60000 * 1024

FFN_ROWS = 1024
FFN_COLS = 512
MIX_ROWS = 512
LN_ROWS = 64
EPILOGUE_ROWS = 256
CONV_ROWS = 64
CONV_HALO = 32
CONV_LANE_CHUNKS = B_WIDTH // V7X_LANES


def _layer_norm(x, g, b):
    mu = jnp.mean(x, axis=-1, keepdims=True)
    xc = x - mu
    var = jnp.mean(xc * xc, axis=-1, keepdims=True)
    return xc * lax.rsqrt(var + LN_EPS) * g + b


def _layer_norm_rows_unrolled(src_ref, dst_ref, g, b, post=None):
    for r in range(src_ref.shape[0] // LN_ROWS):
        sl = slice(r * LN_ROWS, (r + 1) * LN_ROWS)
        y = _layer_norm(src_ref[sl, :].astype(F32), g, b)
        if post is not None:
            y = post(y)
        dst_ref[sl, :] = y.astype(dst_ref.dtype)


def _gelu(x):
    return jax.nn.gelu(x, approximate=True)


def _acc_layer_norm_rows(o_ref, lhs, w_ref, g, b):
    for r in range(o_ref.shape[0] // EPILOGUE_ROWS):
        rows = slice(r * EPILOGUE_ROWS, (r + 1) * EPILOGUE_ROWS)
        y = o_ref[rows, :] + jnp.dot(lhs[rows], w_ref[...], preferred_element_type=F32)
        for q in range(EPILOGUE_ROWS // LN_ROWS):
            sub = slice(q * LN_ROWS, (q + 1) * LN_ROWS)
            o_ref[r * EPILOGUE_ROWS + q * LN_ROWS:r * EPILOGUE_ROWS + (q + 1) * LN_ROWS, :] = (
                _layer_norm(y[sub], g, b))


def _ffn_kernel(n_side, x_ref, wg_ref, wu_ref, wd_ref, g_ref, b_ref, *refs):
    side_in, o_ref, side_out, xb_ref = refs[:n_side], refs[n_side], refs[n_side + 1:2 * n_side + 1], refs[-1]
    j = pl.program_id(1)
    last = pl.num_programs(1) - 1

    def step(is_first, is_last):
        for src_ref, dst_ref in zip(side_in, side_out):
            dst_ref[...] = src_ref[...].astype(BF16)
        if is_first:
            xb_ref[...] = x_ref[...].astype(BF16)
        xb = xb_ref[...]
        hg = jnp.dot(xb, wg_ref[...], preferred_element_type=F32)
        hu = jnp.dot(xb, wu_ref[...], preferred_element_type=F32)
        act = (0.5 * (jax.nn.silu(hg) * hu)).astype(BF16)
        if is_last:
            _acc_layer_norm_rows(o_ref, act, wd_ref, g_ref[...], b_ref[...])
        else:
            d = jnp.dot(act, wd_ref[...], preferred_element_type=F32)
            if is_first:
                o_ref[...] = ALPHA * x_ref[...] + d
            else:
                o_ref[...] += d

    pl.when(j == 0)(functools.partial(step, True, False))
    pl.when(jnp.logical_and(j > 0, j < last))(functools.partial(step, False, False))
    pl.when(j == last)(functools.partial(step, False, True))


def _ffn(x, w_gu, w_down, g, b, side=()):
    m = x.shape[0]
    nj = D_FF // FFN_COLS
    side_specs = [pl.BlockSpec(blk, imap) for _, blk, imap in side]
    outs = pl.pallas_call(
        functools.partial(_ffn_kernel, len(side)),
        grid=(m // FFN_ROWS, nj),
        in_specs=[
            pl.BlockSpec((FFN_ROWS, D_MODEL), lambda i, j: (i, 0)),
            pl.BlockSpec((D_MODEL, FFN_COLS), lambda i, j: (0, j)),
            pl.BlockSpec((D_MODEL, FFN_COLS), lambda i, j: (0, j + nj)),
            pl.BlockSpec((FFN_COLS, D_MODEL), lambda i, j: (j, 0)),
            pl.BlockSpec((1, D_MODEL), lambda i, j: (0, 0)),
            pl.BlockSpec((1, D_MODEL), lambda i, j: (0, 0)),
        ] + side_specs,
        out_specs=[pl.BlockSpec((FFN_ROWS, D_MODEL), lambda i, j: (i, 0))] + side_specs,
        out_shape=[jax.ShapeDtypeStruct((m, D_MODEL), F32)]
        + [jax.ShapeDtypeStruct(w.shape, BF16) for w, _, _ in side],
        scratch_shapes=[pltpu.VMEM((FFN_ROWS, D_MODEL), BF16)],
        compiler_params=pltpu.CompilerParams(
            dimension_semantics=("parallel", "arbitrary"),
            vmem_limit_bytes=V7X_VMEM_LIMIT_BYTES),
        name="ffn_swiglu_ln",
    )(x, w_gu, w_gu, w_down, g, b, *[w for w, _, _ in side])
    return outs[0], outs[1:]


def _conv_lane_chunk(zbuf_ref, wdw_ref, bdw_ref, out_ref, c):
    tm = out_ref.shape[0]
    cs = slice(c * V7X_LANES, (c + 1) * V7X_LANES)
    first_off = CONV_HALO - (CONV_WIDTH - 1)
    bias = bdw_ref[:, cs]
    for rb in range(tm // CONV_ROWS):
        base = rb * CONV_ROWS
        acc = None
        for k in range(CONV_WIDTH):
            rows = pl.ds(base + first_off + k, CONV_ROWS, stride=1)
            term = wdw_ref[k:k + 1, cs] * zbuf_ref[c, rows, :]
            acc = term if acc is None else acc + term
        out_ref[base:base + CONV_ROWS, cs] = acc + bias


def _mixer_branches_kernel(tiles_per_seq, x_ref, w_ref, bin_ref,
                           sg_ref, sb_ref, ws_ref, bs_ref, wdw_ref, bdw_ref, cg_ref, cb_ref,
                           a_out, b_out, xb_ref, u_ref, v_ref, vb_ref, zbuf_ref, conv_ref):
    i = pl.program_id(0)
    tm = x_ref.shape[0]
    lanes_per_sub = V7X_MXU_COLS // V7X_LANES
    u_col, v_col, a_col, g_col = 0, A_WIDTH, 2 * A_WIDTH, 2 * A_WIDTH + B_WIDTH

    def project(col0):
        cs = slice(col0, col0 + V7X_MXU_COLS)
        return jnp.dot(xb_ref[...], w_ref[:, cs], preferred_element_type=F32) + bin_ref[:, cs]

    def project_gelu(dst_ref, col0, n):
        p = project(col0 + n * V7X_MXU_COLS)
        for r in range(tm // LN_ROWS):
            rs = slice(r * LN_ROWS, (r + 1) * LN_ROWS)
            dst_ref[rs, n * V7X_MXU_COLS:(n + 1) * V7X_MXU_COLS] = _gelu(p[rs])

    def project_glu(n):
        seq_start = (i % tiles_per_seq) == 0
        a = project(a_col + n * V7X_MXU_COLS)
        g = project(g_col + n * V7X_MXU_COLS)
        for cc in range(lanes_per_sub):
            c = n * lanes_per_sub + cc
            ls = slice(cc * V7X_LANES, (cc + 1) * V7X_LANES)
            tail = zbuf_ref[c, tm:tm + CONV_HALO, :]
            zbuf_ref[c, 0:CONV_HALO, :] = jnp.where(seq_start, 0.0, tail)
            for r in range(tm // LN_ROWS):
                rs = slice(r * LN_ROWS, (r + 1) * LN_ROWS)
                zs = slice(CONV_HALO + r * LN_ROWS, CONV_HALO + (r + 1) * LN_ROWS)
                zbuf_ref[c, zs, :] = a[rs, ls] * jax.nn.sigmoid(g[rs, ls])

    def conv(c):
        _conv_lane_chunk(zbuf_ref, wdw_ref, bdw_ref, conv_ref, c)

    @pl.when(i == 0)
    def _():
        zbuf_ref[...] = jnp.zeros_like(zbuf_ref)

    xb_ref[...] = x_ref[...].astype(BF16)
    project_glu(0)
    project_glu(1)
    conv(0)
    conv(1)
    project_glu(2)
    conv(2)
    conv(3)
    project_glu(3)
    conv(4)
    project_gelu(u_ref, u_col, 0)
    project_gelu(u_ref, u_col, 1)
    conv(5)
    project_gelu(u_ref, u_col, 2)
    conv(6)
    project_gelu(u_ref, u_col, 3)
    project_gelu(v_ref, v_col, 0)
    project_gelu(v_ref, v_col, 1)
    conv(7)
    project_gelu(v_ref, v_col, 2)
    _layer_norm_rows_unrolled(conv_ref, b_out, cg_ref[...], cb_ref[...], post=jax.nn.silu)
    project_gelu(v_ref, v_col, 3)
    _layer_norm_rows_unrolled(v_ref, vb_ref, sg_ref[...], sb_ref[...])
    row = lax.broadcasted_iota(jnp.int32, (GMLP_BLOCK, GMLP_BLOCK), 0)
    col = lax.broadcasted_iota(jnp.int32, (GMLP_BLOCK, GMLP_BLOCK), 1)
    keep = (col // CHUNK) <= (row // CHUNK)
    for h in range(A_GROUPS):
        wm = jnp.where(keep, ws_ref[h], 0.0).astype(BF16)
        bias = bs_ref[h]
        cs = slice(h * A_HEAD, (h + 1) * A_HEAD)
        for blk in range(tm // GMLP_BLOCK):
            rs = slice(blk * GMLP_BLOCK, (blk + 1) * GMLP_BLOCK)
            s = jnp.dot(wm, vb_ref[rs, cs], preferred_element_type=F32) + bias
            a_out[rs, cs] = (u_ref[rs, cs] * s).astype(a_out.dtype)


def _mixer_branches(x, w_in, b_in, sgu_g, sgu_b, w_s, b_s, w_dw, b_dw, conv_g, conv_b, seq):
    m = x.shape[0]
    tm = MIX_ROWS
    branch_cols = 2 * A_WIDTH + 2 * B_WIDTH
    full = lambda shape: pl.BlockSpec(shape, lambda i: (0,) * len(shape), pipeline_mode=pl.Buffered(1))
    kernel = functools.partial(_mixer_branches_kernel, seq // tm)
    return pl.pallas_call(
        kernel,
        grid=(m // tm,),
        in_specs=[
            pl.BlockSpec((tm, D_MODEL), lambda i: (i, 0)),
            full((D_MODEL, branch_cols)),
            full((1, branch_cols)),
            full((1, A_WIDTH)), full((1, A_WIDTH)),
            full((A_GROUPS, GMLP_BLOCK, GMLP_BLOCK)),
            full((A_GROUPS, GMLP_BLOCK, A_HEAD)),
            full((CONV_WIDTH, B_WIDTH)), full((1, B_WIDTH)),
            full((1, B_WIDTH)), full((1, B_WIDTH)),
        ],
        out_specs=[
            pl.BlockSpec((tm, A_WIDTH), lambda i: (i, 0)),
            pl.BlockSpec((tm, B_WIDTH), lambda i: (i, 0)),
        ],
        out_shape=[
            jax.ShapeDtypeStruct((m, A_WIDTH), BF16),
            jax.ShapeDtypeStruct((m, B_WIDTH), BF16),
        ],
        scratch_shapes=[
            pltpu.VMEM((tm, D_MODEL), BF16),
            pltpu.VMEM((tm, A_WIDTH), F32),
            pltpu.VMEM((tm, A_WIDTH), F32),
            pltpu.VMEM((tm, A_WIDTH), BF16),
            pltpu.VMEM((CONV_LANE_CHUNKS, tm + CONV_HALO, V7X_LANES), F32),
            pltpu.VMEM((tm, B_WIDTH), F32),
        ],
        compiler_params=pltpu.CompilerParams(
            dimension_semantics=("arbitrary",),
            vmem_limit_bytes=V7X_VMEM_LIMIT_BYTES),
        name="mixer_branches",
    )(x, w_in, b_in, sgu_g, sgu_b, w_s, b_s, w_dw, b_dw, conv_g, conv_b)


def _mixer_merge_kernel(x_ref, a_ref, b_ref, wa_ref, wb_ref, wg_ref, bg_ref, wo_ref, g_ref, bt_ref,
                        o_ref, xb_ref, mix_ref):
    tm = x_ref.shape[0]
    xb_ref[...] = x_ref[...].astype(BF16)
    for c in range(D_MODEL // V7X_MXU_COLS):
        cs = slice(c * V7X_MXU_COLS, (c + 1) * V7X_MXU_COLS)
        gs = slice(D_MODEL + c * V7X_MXU_COLS, D_MODEL + (c + 1) * V7X_MXU_COLS)
        la = jnp.dot(xb_ref[...], wg_ref[:, cs], preferred_element_type=F32) + bg_ref[:, cs]
        lb = jnp.dot(xb_ref[...], wg_ref[:, gs], preferred_element_type=F32) + bg_ref[:, gs]
        ya = jnp.dot(a_ref[...], wa_ref[:, cs], preferred_element_type=F32)
        yb = jnp.dot(b_ref[...], wb_ref[:, cs], preferred_element_type=F32)
        for r in range(tm // LN_ROWS):
            rs = slice(r * LN_ROWS, (r + 1) * LN_ROWS)
            mix = jax.nn.sigmoid(la[rs]) * ya[rs] + jax.nn.sigmoid(lb[rs]) * yb[rs]
            mix_ref[rs, cs] = mix.astype(BF16)
    g = g_ref[...]
    bt = bt_ref[...]
    for r in range(tm // EPILOGUE_ROWS):
        rows = slice(r * EPILOGUE_ROWS, (r + 1) * EPILOGUE_ROWS)
        y = ALPHA * x_ref[rows, :] + jnp.dot(mix_ref[rows, :], wo_ref[...], preferred_element_type=F32)
        for q in range(EPILOGUE_ROWS // LN_ROWS):
            sub = slice(q * LN_ROWS, (q + 1) * LN_ROWS)
            o_ref[r * EPILOGUE_ROWS + q * LN_ROWS:r * EPILOGUE_ROWS + (q + 1) * LN_ROWS, :] = (
                _layer_norm(y[sub], g, bt))


def _mixer_merge(x, a, b, w_a, w_b, w_in, b_in, w_out, g, bt):
    m = x.shape[0]
    tm = MIX_ROWS
    full = lambda shape: pl.BlockSpec(shape, lambda i: (0,) * len(shape), pipeline_mode=pl.Buffered(1))
    gates = lambda shape: pl.BlockSpec(shape, lambda i: (0, 1), pipeline_mode=pl.Buffered(1))
    return pl.pallas_call(
        _mixer_merge_kernel,
        grid=(m // tm,),
        in_specs=[
            pl.BlockSpec((tm, D_MODEL), lambda i: (i, 0)),
            pl.BlockSpec((tm, A_WIDTH), lambda i: (i, 0)),
            pl.BlockSpec((tm, B_WIDTH), lambda i: (i, 0)),
            full((A_WIDTH, D_MODEL)),
            full((B_WIDTH, D_MODEL)),
            gates((D_MODEL, 2 * D_MODEL)),
            gates((1, 2 * D_MODEL)),
            full((D_MODEL, D_MODEL)),
            full((1, D_MODEL)),
            full((1, D_MODEL)),
        ],
        out_specs=pl.BlockSpec((tm, D_MODEL), lambda i: (i, 0)),
        out_shape=jax.ShapeDtypeStruct((m, D_MODEL), F32),
        scratch_shapes=[
            pltpu.VMEM((tm, D_MODEL), BF16),
            pltpu.VMEM((tm, D_MODEL), BF16),
        ],
        compiler_params=pltpu.CompilerParams(
            dimension_semantics=("parallel",),
            vmem_limit_bytes=V7X_VMEM_LIMIT_BYTES),
        name="mixer_merge",
    )(x, a, b, w_a, w_b, w_in, b_in, w_out, g, bt)


def kernel(x, ffn1_w_gu, ffn1_w_down, ln1_g, ln1_b, w_in, b_in, sgu_ln_g, sgu_ln_b, sgu_w_s, sgu_b_s, w_a_proj, conv_w_dw, conv_b_dw, conv_ln_g, conv_ln_b, w_b_proj, w_out, ln2_g, ln2_b, ffn2_w_gu, ffn2_w_down, ln3_g, ln3_b):
    bsz, seq, _ = x.shape
    h = x.reshape(bsz * seq, D_MODEL)
    row = lambda v: v.reshape(1, -1)

    n_tiles = (bsz * seq) // FFN_ROWS
    n_steps = D_FF // FFN_COLS
    side_rows = D_MODEL // n_tiles
    side_cols = 2 * D_FF // n_steps
    in_blocks = w_in.shape[-1] // side_cols
    for l in range(DEPTH):
        side = (
            (ffn2_w_gu[l], (side_rows, side_cols), lambda i, j: (i, j)),
            (ffn2_w_down[l], (D_FF // (n_tiles * n_steps), D_MODEL), lambda i, j: (i * n_steps + j, 0)),
            (w_in[l], (side_rows, side_cols), lambda i, j: (i, jnp.minimum(j, in_blocks - 1))),
        )
        h, (w_gu2, w_down2, w_in_b) = _ffn(h, ffn1_w_gu[l].astype(BF16), ffn1_w_down[l].astype(BF16),
                                           row(ln1_g[l]), row(ln1_b[l]), side)
        b_in_r = row(b_in[l])
        b_s = jnp.broadcast_to(sgu_b_s[l][:, :, None], (A_GROUPS, GMLP_BLOCK, A_HEAD))
        a, b = _mixer_branches(h, w_in_b, b_in_r, row(sgu_ln_g[l]), row(sgu_ln_b[l]),
                               sgu_w_s[l], b_s, conv_w_dw[l], row(conv_b_dw[l]),
                               row(conv_ln_g[l]), row(conv_ln_b[l]), seq)
        h = _mixer_merge(h, a, b, w_a_proj[l].astype(BF16), w_b_proj[l].astype(BF16),
                         w_in_b, b_in_r, w_out[l].astype(BF16), row(ln2_g[l]), row(ln2_b[l]))
        h, _ = _ffn(h, w_gu2, w_down2, row(ln3_g[l]), row(ln3_b[l]))
    return h.reshape(bsz, seq, D_MODEL)
```
